```python
import jax, jax.numpy as jnp
from jax import lax
import numpy as np

D_MODEL = 1024
BATCH = 8
SEQ = 2048
DEPTH = 4
DEC_BATCH = 32
DEC_SEQ = 1
PAST_LEN = 16384
PAGE_SIZE = 128

MIX_WIDTH = D_MODEL
MLA_HEADS = 6
MLA_NOPE = 64
MLA_ROPE = 32
MLA_V = 64
MLA_Q_LORA = 256
MLA_KV_LORA = 128
MLA_WIDTH = MLA_HEADS * MLA_V
ROPE_THETA = 10000.0
S5_CH = 16
S5_GROUPS = 24
S5_STATE = 64
S5_WIDTH = S5_GROUPS * S5_CH
SB_HEADS = 4
SB_HEAD_DIM = 64
SB_WIDTH = SB_HEADS * SB_HEAD_DIM
Q_BLOCK = 128
IN_SPLITS = (MLA_Q_LORA, MLA_KV_LORA, MLA_ROPE, S5_WIDTH, SB_WIDTH, SB_WIDTH, SB_WIDTH)
IN_COLS = sum(IN_SPLITS)
IN_OFFSETS = tuple(int(v) for v in np.cumsum(IN_SPLITS)[:-1])
N_EXPERTS = 32
TOP_K = 4
D_FF = 1024
SWIGLU_LIMIT = 7.0
SWIGLU_ALPHA = 1.702
MOE_BLOCK = 128
N_MOD = 6
EPS = 1e-6

kernel_name = 'hymba_mla_s5_stickbreak_moe_step'


def _rms(x, g):
    xf = x.astype(jnp.float32)
    y = xf * lax.rsqrt(jnp.mean(xf * xf, axis=-1, keepdims=True) + EPS)
    return (y * g.astype(jnp.float32)).astype(x.dtype)


def _rope(x, pos):
    half = x.shape[-1] // 2
    inv = ROPE_THETA ** (-jnp.arange(half, dtype=jnp.float32) / half)
    ang = pos.astype(jnp.float32)[:, None] * inv
    ang = ang.reshape((ang.shape[0],) + (1,) * (x.ndim - 3) + (half,))
    cos, sin = jnp.cos(ang), jnp.sin(ang)
    x1, x2 = x[..., :half], x[..., half:]
    return jnp.concatenate([x1 * cos - x2 * sin, x1 * sin + x2 * cos], axis=-1).astype(x.dtype)


def _map_query_blocks(fn, qs, q_pos):
    T = q_pos.shape[0]
    blk = min(Q_BLOCK, T)
    n = -(-T // blk)
    pad = n * blk - T

    def prep(a):
        a = jnp.pad(a, [(0, 0), (0, pad)] + [(0, 0)] * (a.ndim - 2))
        return jnp.moveaxis(a.reshape((a.shape[0], n, blk) + a.shape[2:]), 1, 0)

    pos = jnp.pad(q_pos, (0, pad), mode='edge').reshape(n, blk)
    out = lax.map(fn, (tuple(prep(a) for a in qs), pos))
    out = jnp.moveaxis(out, 0, 1)
    out = out.reshape((out.shape[0], n * blk) + out.shape[3:])
    return out[:, :T]


def _mla_attend(q_nope, q_pe, k_nope, k_pe, v, q_pos, k_pos):
    scale = (MLA_NOPE + MLA_ROPE) ** -0.5

    def block(args):
        (qn_b, qp_b), pos_b = args
        s = jnp.einsum('bqhd,bkhd->bhqk', qn_b, k_nope) + jnp.einsum('bqhr,bkr->bhqk', qp_b, k_pe)
        s = s.astype(jnp.float32) * scale
        s = jnp.where(k_pos[None, :] <= pos_b[:, None], s, -jnp.inf)
        p = jax.nn.softmax(s, axis=-1)
        return jnp.einsum('bhqk,bkhd->bqhd', p.astype(v.dtype), v)

    return _map_query_blocks(block, (q_nope, q_pe), q_pos)


def _sb_attend(q, k, v, q_pos, k_pos):
    scale = SB_HEAD_DIM ** -0.5

    def block(args):
        (q_b,), pos_b = args
        z = jnp.einsum('bqhd,bkhd->bhqk', q_b, k).astype(jnp.float32) * scale
        mask = k_pos[None, :] < pos_b[:, None]
        log_beta = jnp.where(mask, jax.nn.log_sigmoid(z), -jnp.inf)
        log_keep = jnp.where(mask, jax.nn.log_sigmoid(-z), 0.0)
        after = lax.cumsum(log_keep, axis=3, reverse=True) - log_keep
        w = jnp.exp(log_beta + after)
        return jnp.einsum('bhqk,bkhd->bqhd', w.astype(v.dtype), v)

    return _map_query_blocks(block, (q,), q_pos)


def _s5(u, h0_re, h0_im, lam_re, lam_im, log_step, b_re, b_im, c_re, c_im, d_skip, w_glu, b_glu):
    Bsz, T, _ = u.shape
    uf = u.astype(jnp.float32).reshape(Bsz, T, S5_GROUPS, S5_CH)
    lr, li = lam_re.astype(jnp.float32), lam_im.astype(jnp.float32)
    dt = jnp.exp(log_step.astype(jnp.float32))[:, None]
    mag = jnp.exp(lr * dt)
    a_re, a_im = mag * jnp.cos(li * dt), mag * jnp.sin(li * dt)
    nr, ni = a_re - 1.0, a_im
    den = lr * lr + li * li
    f_re, f_im = (nr * lr + ni * li) / den, (ni * lr - nr * li) / den
    br, bi = b_re.astype(jnp.float32), b_im.astype(jnp.float32)
    bb_re = f_re[..., None] * br - f_im[..., None] * bi
    bb_im = f_re[..., None] * bi + f_im[..., None] * br
    bu_re = jnp.einsum('btgc,gnc->btgn', uf, bb_re)
    bu_im = jnp.einsum('btgc,gnc->btgn', uf, bb_im)
    hr0, hi0 = h0_re.astype(jnp.float32), h0_im.astype(jnp.float32)
    bu_re = bu_re.at[:, 0].add(a_re * hr0 - a_im * hi0)
    bu_im = bu_im.at[:, 0].add(a_re * hi0 + a_im * hr0)
    ar = jnp.broadcast_to(a_re, bu_re.shape)
    ai = jnp.broadcast_to(a_im, bu_re.shape)

    def combine(e1, e2):
        a1r, a1i, b1r, b1i = e1
        a2r, a2i, b2r, b2i = e2
        return (a1r * a2r - a1i * a2i, a1r * a2i + a1i * a2r,
                a2r * b1r - a2i * b1i + b2r, a2r * b1i + a2i * b1r + b2i)

    _, _, hr, hi = lax.associative_scan(combine, (ar, ai, bu_re, bu_im), axis=1)
    y = (jnp.einsum('btgn,gcn->btgc', hr, c_re.astype(jnp.float32))
         - jnp.einsum('btgn,gcn->btgc', hi, c_im.astype(jnp.float32))
         + d_skip.astype(jnp.float32).reshape(S5_GROUPS, S5_CH) * uf)
    y = jax.nn.gelu(y.reshape(Bsz, T, S5_WIDTH))
    y = y * jax.nn.sigmoid(y @ w_glu.astype(jnp.float32) + b_glu.astype(jnp.float32))
    return y.astype(u.dtype), hr[:, -1], hi[:, -1]


def _moe(h, w_router, b_router, w_gu, b_gu, w_down, b_down):
    T, D = h.shape
    logits = (h @ w_router).astype(jnp.float32) + b_router.astype(jnp.float32)
    top_val, top_idx = lax.top_k(logits, TOP_K)
    gates = jax.nn.softmax(top_val, axis=-1)
    tk = T * TOP_K
    blk = max(1, min(MOE_BLOCK, tk // N_EXPERTS))
    n_blocks = -(-tk // blk) + N_EXPERTS
    flat_e = top_idx.reshape(tk)
    flat_tok = jnp.repeat(jnp.arange(T, dtype=jnp.int32), TOP_K)
    flat_g = gates.reshape(tk)
    order = jnp.argsort(flat_e)
    e_sorted = flat_e[order]
    counts = jnp.zeros((N_EXPERTS,), jnp.int32).at[flat_e].add(1)
    blocks_per_e = (counts + blk - 1) // blk
    block_end = jnp.cumsum(blocks_per_e)
    group_start = jnp.cumsum(counts) - counts
    dest = ((block_end - blocks_per_e)[e_sorted] * blk
            + jnp.arange(tk, dtype=jnp.int32) - group_start[e_sorted])
    slot_tok = jnp.zeros((n_blocks * blk,), jnp.int32).at[dest].set(flat_tok[order])
    slot_gate = jnp.zeros((n_blocks * blk,), jnp.float32).at[dest].set(flat_g[order])
    block_expert = jnp.minimum(
        jnp.searchsorted(block_end, jnp.arange(n_blocks, dtype=jnp.int32), side='right'), N_EXPERTS - 1)
    xb = h[slot_tok].reshape(n_blocks, blk, D)

    def expert_block(args):
        xe, e = args
        gu = xe @ w_gu[e] + b_gu[e]
        g = jnp.minimum(gu[:, 0::2], SWIGLU_LIMIT)
        up = jnp.clip(gu[:, 1::2], -SWIGLU_LIMIT, SWIGLU_LIMIT)
        act = (up + 1) * (g * jax.nn.sigmoid(SWIGLU_ALPHA * g))
        return act @ w_down[e] + b_down[e]

    yb = lax.map(expert_block, (xb, block_expert))
    contrib = yb.reshape(-1, D).astype(jnp.float32) * slot_gate[:, None]
    return jnp.zeros((T, D), jnp.float32).at[slot_tok].add(contrib).astype(h.dtype)


def _gather_pages(pool, l, page_table):
    g = pool[l, page_table]
    return g.reshape((g.shape[0], g.shape[1] * g.shape[2]) + g.shape[3:])


def _layer(x, c, pos_q, past, lp):
    Bsz, T, _ = x.shape
    mod = jax.nn.silu(c) @ lp['w_ada'] + lp['b_ada']
    sh_a, sc_a, gt_a, sh_f, sc_f, gt_f = jnp.split(mod[:, None, :], N_MOD, axis=-1)
    h = _rms(x, lp['g_mix']) * (1 + sc_a) + sh_a
    z = h @ lp['w_in']
    q_lat, kv_lat, k_pe_raw, u_s5, sb_q, sb_k, sb_v = jnp.split(z, IN_OFFSETS, axis=-1)
    q = (_rms(q_lat, lp['g_q_lat']) @ lp['w_uq']).reshape(Bsz, T, MLA_HEADS, MLA_NOPE + MLA_ROPE)
    q_nope = _rms(q[..., :MLA_NOPE], lp['g_qn_nope'])
    q_pe = _rope(_rms(q[..., MLA_NOPE:], lp['g_qn_rope']), pos_q)
    c_kv = _rms(kv_lat, lp['g_kv_lat'])
    k_pe = _rope(_rms(k_pe_raw, lp['g_kn_rope']), pos_q)
    sb_q = sb_q.reshape(Bsz, T, SB_HEADS, SB_HEAD_DIM)
    sb_k = sb_k.reshape(Bsz, T, SB_HEADS, SB_HEAD_DIM)
    sb_v = sb_v.reshape(Bsz, T, SB_HEADS, SB_HEAD_DIM)
    if past is None:
        c_all, kpe_all, sbk_all, sbv_all = c_kv, k_pe, sb_k, sb_v
        h0_re = jnp.zeros((Bsz, S5_GROUPS, S5_STATE), jnp.float32)
        h0_im = jnp.zeros((Bsz, S5_GROUPS, S5_STATE), jnp.float32)
    else:
        pc, pk, psk, psv, h0_re, h0_im = past
        c_all = jnp.concatenate([pc.astype(c_kv.dtype), c_kv], axis=1)
        kpe_all = jnp.concatenate([pk.astype(k_pe.dtype), k_pe], axis=1)
        sbk_all = jnp.concatenate([psk.astype(sb_k.dtype), sb_k], axis=1)
        sbv_all = jnp.concatenate([psv.astype(sb_v.dtype), sb_v], axis=1)
    k_pos = jnp.arange(c_all.shape[1], dtype=jnp.int32)
    k_nope = _rms((c_all @ lp['w_uk']).reshape(Bsz, -1, MLA_HEADS, MLA_NOPE), lp['g_kn_nope'])
    v = (c_all @ lp['w_uv']).reshape(Bsz, -1, MLA_HEADS, MLA_V)
    o_a = _mla_attend(q_nope, q_pe, k_nope, kpe_all, v, pos_q, k_pos).reshape(Bsz, T, MLA_WIDTH)
    o_b, s_re, s_im = _s5(u_s5, h0_re, h0_im, lp['s5_lam_re'], lp['s5_lam_im'], lp['s5_log_step'],
                          lp['s5_b_re'], lp['s5_b_im'], lp['s5_c_re'], lp['s5_c_im'], lp['s5_d'],
                          lp['w_glu'], lp['b_glu'])
    o_c = _sb_attend(sb_q, sbk_all, sbv_all, pos_q, k_pos).reshape(Bsz, T, SB_WIDTH)
    o = jnp.concatenate([o_a, o_b, o_c], axis=-1) @ lp['w_out']
    x = x + gt_a * o
    h2 = _rms(x, lp['g_ffn']) * (1 + sc_f) + sh_f
    y = _moe(h2.reshape(Bsz * T, D_MODEL), lp['w_router'], lp['b_router'], lp['w_gu'], lp['b_gu'],
             lp['w_down'], lp['b_down']).reshape(x.shape)
    x = x + gt_f * y
    return x, (c_kv, k_pe, sb_k, sb_v, s_re, s_im)


def setup_inputs(seed: int = 0) -> dict:
    key = jax.random.key(seed)
    keys = list(jax.random.split(key, 64))

    def nrm(shape, scale=1.0):
        return jax.random.normal(keys.pop(), shape, jnp.float32) * scale

    def gain(n):
        return 1.0 + nrm((DEPTH, n), 0.05)

    n_pages = PAST_LEN // PAGE_SIZE
    n_pool = (DEC_BATCH * n_pages * 5 + 3) // 4
    perm = jax.random.permutation(keys.pop(), n_pool)[: DEC_BATCH * n_pages]
    page_table = perm.reshape(DEC_BATCH, n_pages).astype(jnp.int32)
    lam_im = jnp.broadcast_to(jnp.pi * jnp.arange(S5_STATE, dtype=jnp.float32), (DEPTH, S5_GROUPS, S5_STATE))
    log_step = jax.random.uniform(keys.pop(), (DEPTH, S5_GROUPS), jnp.float32,
                                  float(np.log(0.001)), float(np.log(0.1)))
    return {
        'x_prompt': nrm((BATCH, SEQ, D_MODEL)),
        'x_sample': nrm((DEC_BATCH, DEC_SEQ, D_MODEL)),
        'cache_kv_latent': nrm((DEPTH, n_pool, PAGE_SIZE, MLA_KV_LORA)),
        'cache_k_rope': nrm((DEPTH, n_pool, PAGE_SIZE, MLA_ROPE)),
        'cache_sb_k': nrm((DEPTH, n_pool, PAGE_SIZE, SB_HEADS, SB_HEAD_DIM)),
        'cache_sb_v': nrm((DEPTH, n_pool, PAGE_SIZE, SB_HEADS, SB_HEAD_DIM)),
        'state_s5_re': nrm((DEPTH, DEC_BATCH, S5_GROUPS, S5_STATE), 0.3),
        'state_s5_im': nrm((DEPTH, DEC_BATCH, S5_GROUPS, S5_STATE), 0.3),
        'page_table': page_table,
        'c_prompt': nrm((BATCH, D_MODEL)),
        'c_sample': nrm((DEC_BATCH, D_MODEL)),
        'w_ada': nrm((DEPTH, D_MODEL, N_MOD * D_MODEL), 0.5 * D_MODEL ** -0.5),
        'b_ada': nrm((DEPTH, N_MOD * D_MODEL), 0.02),
        'g_mix': gain(D_MODEL),
        'w_in': nrm((DEPTH, D_MODEL, IN_COLS), D_MODEL ** -0.5),
        'g_q_lat': gain(MLA_Q_LORA),
        'w_uq': nrm((DEPTH, MLA_Q_LORA, MLA_HEADS * (MLA_NOPE + MLA_ROPE)), MLA_Q_LORA ** -0.5),
        'g_qn_nope': gain(MLA_NOPE),
        'g_qn_rope': gain(MLA_ROPE),
        'g_kv_lat': gain(MLA_KV_LORA),
        'g_kn_rope': gain(MLA_ROPE),
        'w_uk': nrm((DEPTH, MLA_KV_LORA, MLA_HEADS * MLA_NOPE), MLA_KV_LORA ** -0.5),
        'w_uv': nrm((DEPTH, MLA_KV_LORA, MLA_HEADS * MLA_V), MLA_KV_LORA ** -0.5),
        'g_kn_nope': gain(MLA_NOPE),
        's5_lam_re': -0.5 * (1.0 + nrm((DEPTH, S5_GROUPS, S5_STATE), 0.02)),
        's5_lam_im': lam_im,
        's5_log_step': log_step,
        's5_b_re': nrm((DEPTH, S5_GROUPS, S5_STATE, S5_CH), (2 * S5_CH) ** -0.5),
        's5_b_im': nrm((DEPTH, S5_GROUPS, S5_STATE, S5_CH), (2 * S5_CH) ** -0.5),
        's5_c_re': nrm((DEPTH, S5_GROUPS, S5_CH, S5_STATE), S5_STATE ** -0.5),
        's5_c_im': nrm((DEPTH, S5_GROUPS, S5_CH, S5_STATE), S5_STATE ** -0.5),
        's5_d': nrm((DEPTH, S5_WIDTH)),
        'w_glu': nrm((DEPTH, S5_WIDTH, S5_WIDTH), S5_WIDTH ** -0.5),
        'b_glu': nrm((DEPTH, S5_WIDTH), 0.02),
        'w_out': nrm((DEPTH, MIX_WIDTH, D_MODEL), MIX_WIDTH ** -0.5),
        'g_ffn': gain(D_MODEL),
        'w_router': nrm((DEPTH, D_MODEL, N_EXPERTS), D_MODEL ** -0.5),
        'b_router': nrm((DEPTH, N_EXPERTS), 0.01),
        'w_gu': nrm((DEPTH, N_EXPERTS, D_MODEL, 2 * D_FF), D_MODEL ** -0.5),
        'b_gu': nrm((DEPTH, N_EXPERTS, 2 * D_FF), 0.01),
        'w_down': nrm((DEPTH, N_EXPERTS, D_FF, D_MODEL), D_FF ** -0.5),
        'b_down': nrm((DEPTH, N_EXPERTS, D_MODEL), 0.01),
    }


def reference(x_prompt, x_sample, cache_kv_latent, cache_k_rope, cache_sb_k, cache_sb_v,
              state_s5_re, state_s5_im, page_table, c_prompt, c_sample,
              w_ada, b_ada, g_mix, w_in, g_q_lat, w_uq, g_qn_nope, g_qn_rope, g_kv_lat, g_kn_rope,
              w_uk, w_uv, g_kn_nope, s5_lam_re, s5_lam_im, s5_log_step, s5_b_re, s5_b_im,
              s5_c_re, s5_c_im, s5_d, w_glu, b_glu, w_out, g_ffn, w_router, b_router,
              w_gu, b_gu, w_down, b_down):
    past_len = page_table.shape[1] * PAGE_SIZE
    pos_p = jnp.arange(x_prompt.shape[1], dtype=jnp.int32)
    pos_s = past_len + jnp.arange(x_sample.shape[1], dtype=jnp.int32)
    xp, xs = x_prompt, x_sample
    p_rows, s_rows = [], []
    for l in range(DEPTH):
        lp = dict(w_ada=w_ada[l], b_ada=b_ada[l], g_mix=g_mix[l], w_in=w_in[l], g_q_lat=g_q_lat[l],
                  w_uq=w_uq[l], g_qn_nope=g_qn_nope[l], g_qn_rope=g_qn_rope[l], g_kv_lat=g_kv_lat[l],
                  g_kn_rope=g_kn_rope[l], w_uk=w_uk[l], w_uv=w_uv[l], g_kn_nope=g_kn_nope[l],
                  s5_lam_re=s5_lam_re[l], s5_lam_im=s5_lam_im[l], s5_log_step=s5_log_step[l],
                  s5_b_re=s5_b_re[l], s5_b_im=s5_b_im[l], s5_c_re=s5_c_re[l], s5_c_im=s5_c_im[l],
                  s5_d=s5_d[l], w_glu=w_glu[l], b_glu=b_glu[l], w_out=w_out[l], g_ffn=g_ffn[l],
                  w_router=w_router[l], b_router=b_router[l], w_gu=w_gu[l], b_gu=b_gu[l],
                  w_down=w_down[l], b_down=b_down[l])
        xp, st_p = _layer(xp, c_prompt, pos_p, None, lp)
        p_rows.append(st_p)
        past = (_gather_pages(cache_kv_latent, l, page_table), _gather_pages(cache_k_rope, l, page_table),
                _gather_pages(cache_sb_k, l, page_table), _gather_pages(cache_sb_v, l, page_table),
                state_s5_re[l], state_s5_im[l])
        xs, st_s = _layer(xs, c_sample, pos_s, past, lp)
        s_rows.append(st_s)
    p_kv, p_kr, p_sk, p_sv, p_re, p_im = [jnp.stack(a) for a in zip(*p_rows)]
    s_kv, s_kr, s_sk, s_sv, s_re, s_im = [jnp.stack(a) for a in zip(*s_rows)]
    return (xp, xs, p_kv, p_kr, p_sk, p_sv, p_re, p_im, s_kv, s_kr, s_sk, s_sv, s_re, s_im)
```

```python
import functools

import jax
import jax.numpy as jnp
import numpy as np
from jax import lax
from jax.experimental import pallas as pl
from jax.experimental.pallas import tpu as pltpu

F32 = jnp.float32
BF16 = jnp.bfloat16

D_MODEL = 1024
PAGE_SIZE = 128
MLA_HEADS = 6
MLA_NOPE = 64
MLA_ROPE = 32
MLA_V = 64
MLA_Q_LORA = 256
MLA_KV_LORA = 128
ROPE_THETA = 10000.0
S5_CH = 16
S5_GROUPS = 24
S5_STATE = 64
S5_WIDTH = S5_GROUPS * S5_CH
S5_LANES = S5_GROUPS * S5_STATE
SB_HEADS = 4
SB_HEAD_DIM = 64
SB_WIDTH = SB_HEADS * SB_HEAD_DIM
N_EXPERTS = 32
TOP_K = 4
D_FF = 1024
SWIGLU_LIMIT = 7.0
SWIGLU_ALPHA = 1.702
N_MOD = 6
EPS = 1e-6

LANE = 128
HEAD_PAD = 128
S5_BLK_GROUPS = LANE // S5_CH
S5_NBLK = S5_GROUPS // S5_BLK_GROUPS
S5_BLK_LANES = S5_BLK_GROUPS * S5_STATE
ROUTER_PAD = 128
NEG_BIG = -1e30
VMEM_LIMIT = 56 * 1024 * 1024

_C_Q, _C_KV, _C_KPE, _C_U, _C_SQ, _C_SK, _C_SV, _C_END = 0, 256, 384, 512, 896, 1152, 1408, 1664


def _cparams(sem):
    return pltpu.CompilerParams(dimension_semantics=sem, vmem_limit_bytes=VMEM_LIMIT)


def _dot(a, b):
    return jnp.dot(a, b, preferred_element_type=F32)


def _dot_nt(a, b):
    return lax.dot_general(a, b, (((1,), (1,)), ((), ())), preferred_element_type=F32)


def _rms(x, g):
    return x * lax.rsqrt(jnp.mean(x * x, axis=-1, keepdims=True) + EPS) * g


def _rope_lanes(x, cos, sin_a, sin_b):
    return x * cos + pltpu.roll(x, HEAD_PAD - MLA_ROPE // 2, 1) * sin_a + pltpu.roll(x, MLA_ROPE // 2, 1) * sin_b


def _ada_body(c_ref, w_ref, b_ref, o_ref):
    c = c_ref[...]
    s = (c * jax.nn.sigmoid(c)).astype(BF16)
    o_ref[0] = _dot(s, w_ref[0].astype(BF16)) + b_ref[0]


def _ada(c_all, w_ada, b_ada):
    L = w_ada.shape[0]
    R = c_all.shape[0]
    ncol = N_MOD * D_MODEL // D_MODEL
    return pl.pallas_call(
        _ada_body,
        grid=(L, ncol),
        in_specs=[pl.BlockSpec((R, D_MODEL), lambda l, j: (0, 0)),
                  pl.BlockSpec((1, D_MODEL, D_MODEL), lambda l, j: (l, 0, j)),
                  pl.BlockSpec((1, 1, D_MODEL), lambda l, j: (l, 0, j))],
        out_specs=pl.BlockSpec((1, R, D_MODEL), lambda l, j: (l, 0, j)),
        out_shape=jax.ShapeDtypeStruct((L, R, N_MOD * D_MODEL), F32),
        compiler_params=_cparams(("parallel", "parallel")),
    )(c_all, w_ada, b_ada.reshape(L, 1, N_MOD * D_MODEL))


def _s5disc_body(lr_ref, li_ref, ls_ref, ar_ref, ai_ref, fr_ref, fi_ref):
    lr, li = lr_ref[...], li_ref[...]
    dt = jnp.exp(ls_ref[...])
    mag = jnp.exp(lr * dt)
    a_re, a_im = mag * jnp.cos(li * dt), mag * jnp.sin(li * dt)
    nr, ni = a_re - 1.0, a_im
    den = lr * lr + li * li
    ar_ref[...] = a_re
    ai_ref[...] = a_im
    fr_ref[...] = (nr * lr + ni * li) / den
    fi_ref[...] = (ni * lr - nr * li) / den


def _s5disc(lam_re, lam_im, log_step):
    L, G, N = lam_re.shape
    R = L * G
    sds = jax.ShapeDtypeStruct((R, N), F32)
    outs = pl.pallas_call(_s5disc_body, out_shape=(sds, sds, sds, sds))(
        lam_re.reshape(R, N), lam_im.reshape(R, N), log_step.reshape(R, 1))
    return [o.reshape(L, G, N) for o in outs]


def _proj_body(x_ref, sc_ref, sh_ref, gmix_ref, w1_ref, gql_ref, wuq_ref, gkv_ref, wuk_ref, wuv_ref,
               gq_ref, gk_ref, gkr_ref, cos_ref, sa_ref, sb_ref,
               q_ref, k_ref, v_ref, ckv_ref, kpe_ref, u_ref, sq_ref, sk16_ref, sv16_ref, sk_ref, sv_ref):
    x = x_ref[...]
    hn = _rms(x, gmix_ref[...]) * (1.0 + sc_ref[0]) + sh_ref[0]
    z = _dot(hn.astype(BF16), w1_ref[...])
    tm = x.shape[0]
    lane = lax.broadcasted_iota(jnp.int32, (tm, HEAD_PAD), 1)
    nope = lane < MLA_NOPE
    cos, sin_a, sin_b = cos_ref[...], sa_ref[...], sb_ref[...]

    ckv = _rms(z[:, _C_KV:_C_KPE], gkv_ref[...])
    ckv_ref[...] = ckv
    kraw = z[:, _C_KPE:_C_U]
    kr = kraw * lax.rsqrt(jnp.sum(kraw * kraw, axis=-1, keepdims=True) * (1.0 / MLA_ROPE) + EPS) * gkr_ref[...]
    kr = _rope_lanes(kr, cos, sin_a, sin_b)
    kpe_ref[...] = pltpu.roll(kr, HEAD_PAD - MLA_NOPE, 1)[:, :MLA_ROPE]

    qn = _rms(z[:, _C_Q:_C_KV], gql_ref[...]).astype(BF16)
    qa = _dot(qn, wuq_ref[...])
    cb = ckv.astype(BF16)
    ka = _dot(cb, wuk_ref[...])
    for h in range(MLA_HEADS):
        sl = slice(h * HEAD_PAD, (h + 1) * HEAD_PAD)
        qh = qa[:, sl]
        q2 = qh * qh
        ss_n = jnp.sum(jnp.where(nope, q2, 0.0), axis=-1, keepdims=True) * (1.0 / MLA_NOPE)
        ss_r = jnp.sum(jnp.where(nope, 0.0, q2), axis=-1, keepdims=True) * (1.0 / MLA_ROPE)
        qh = qh * jnp.where(nope, lax.rsqrt(ss_n + EPS), lax.rsqrt(ss_r + EPS)) * gq_ref[...]
        q_ref[:, sl] = _rope_lanes(qh, cos, sin_a, sin_b).astype(q_ref.dtype)
        kh = ka[:, sl]
        kh = kh * lax.rsqrt(jnp.sum(kh * kh, axis=-1, keepdims=True) * (1.0 / MLA_NOPE) + EPS) * gk_ref[...]
        k_ref[:, sl] = (kh + kr).astype(k_ref.dtype)
    v_ref[...] = _dot(cb, wuv_ref[...]).astype(v_ref.dtype)

    u_ref[...] = z[:, _C_U:_C_SQ]
    sq_ref[...] = (z[:, _C_SQ:_C_SK] * (SB_HEAD_DIM ** -0.5)).astype(BF16)
    sk = z[:, _C_SK:_C_SV]
    sv = z[:, _C_SV:_C_END]
    sk_ref[...] = sk
    sv_ref[...] = sv
    sk16_ref[...] = sk.astype(BF16)
    sv16_ref[...] = sv.astype(BF16)


def _proj(x, sc, sh, tabs, lw, *, B, T, tm, qk_dtype):
    N = B * T
    nT = T // tm
    mt = sc.shape[1]
    row = lambda b, t: (b * nT + t, 0)
    full = lambda b, t: (0, 0)
    if mt == 1:
        mod_spec = pl.BlockSpec((1, 1, D_MODEL), lambda b, t: (b, 0, 0))
    else:
        mod_spec = pl.BlockSpec((1, tm, D_MODEL), lambda b, t: (b, t, 0))
    tab_spec = pl.BlockSpec((tm, HEAD_PAD), lambda b, t: (t, 0))

    def wspec(a):
        return pl.BlockSpec(a.shape, full)

    weights = [lw['g_mix'], lw['w1'], lw['g_q_lat'], lw['wuq'], lw['g_kv_lat'], lw['wuk'], lw['wuv'],
               lw['gq'], lw['gk'], lw['gkr']]
    in_specs = ([pl.BlockSpec((tm, D_MODEL), row), mod_spec, mod_spec] + [wspec(w) for w in weights[:1]]
                + [wspec(w) for w in weights[1:]] + [tab_spec] * 3)
    QW = MLA_HEADS * HEAD_PAD
    VW = MLA_HEADS * MLA_V
    out_shape = (
        jax.ShapeDtypeStruct((N, QW), qk_dtype), jax.ShapeDtypeStruct((N, QW), qk_dtype),
        jax.ShapeDtypeStruct((N, VW), BF16),
        jax.ShapeDtypeStruct((N, MLA_KV_LORA), F32), jax.ShapeDtypeStruct((N, MLA_ROPE), F32),
        jax.ShapeDtypeStruct((T, B * S5_WIDTH), F32),
        jax.ShapeDtypeStruct((N, SB_WIDTH), BF16), jax.ShapeDtypeStruct((N, SB_WIDTH), BF16),
        jax.ShapeDtypeStruct((N, SB_WIDTH), BF16),
        jax.ShapeDtypeStruct((N, SB_WIDTH), F32), jax.ShapeDtypeStruct((N, SB_WIDTH), F32))
    out_specs = (
        pl.BlockSpec((tm, QW), row), pl.BlockSpec((tm, QW), row), pl.BlockSpec((tm, VW), row),
        pl.BlockSpec((tm, MLA_KV_LORA), row), pl.BlockSpec((tm, MLA_ROPE), row),
        pl.BlockSpec((tm, S5_WIDTH), lambda b, t: (t, b)),
        pl.BlockSpec((tm, SB_WIDTH), row), pl.BlockSpec((tm, SB_WIDTH), row), pl.BlockSpec((tm, SB_WIDTH), row),
        pl.BlockSpec((tm, SB_WIDTH), row), pl.BlockSpec((tm, SB_WIDTH), row))
    return pl.pallas_call(
        _proj_body, grid=(B, nT), in_specs=in_specs, out_specs=out_specs, out_shape=out_shape,
        compiler_params=_cparams(("parallel", "parallel")),
    )(x, sc, sh, *weights, *tabs)


def _mla_body(q_ref, k_ref, v_ref, o_ref, m_ref, l_ref, acc_ref, *, tq, scale):
    qi = pl.program_id(2)
    m_ref[...] = jnp.full(m_ref.shape, -jnp.inf, F32)
    l_ref[...] = jnp.zeros(l_ref.shape, F32)
    acc_ref[...] = jnp.zeros(acc_ref.shape, F32)
    q = q_ref[...]

    def block(kb, masked):
        r0 = pl.multiple_of(kb * tq, tq)
        k = k_ref[pl.ds(r0, tq), :]
        v = v_ref[pl.ds(r0, tq), :]
        for hh in range(2):
            sl = slice(hh * HEAD_PAD, (hh + 1) * HEAD_PAD)
            s = _dot_nt(q[:, sl], k[:, sl]) * scale
            if masked:
                rowi = lax.broadcasted_iota(jnp.int32, s.shape, 0)
                coli = lax.broadcasted_iota(jnp.int32, s.shape, 1)
                s = jnp.where(coli <= rowi, s, -jnp.inf)
            m_prev = m_ref[hh]
            m_new = jnp.maximum(m_prev, jnp.max(s, axis=-1, keepdims=True))
            alpha = jnp.exp(m_prev - m_new)
            p = jnp.exp(s - m_new[:, :1])
            l_ref[hh] = alpha * l_ref[hh] + jnp.sum(p, axis=-1, keepdims=True)
            acc_ref[hh] = alpha * acc_ref[hh] + _dot(p.astype(BF16), v)
            m_ref[hh] = m_new

    def loop_body(kb, carry):
        block(kb, False)
        return carry

    lax.fori_loop(0, qi, loop_body, 0)
    block(qi, True)
    lane = lax.broadcasted_iota(jnp.int32, (tq, LANE), 1)
    o_ref[...] = jnp.where(lane < MLA_V, acc_ref[0] / l_ref[0], acc_ref[1] / l_ref[1]).astype(o_ref.dtype)


def _mla_prompt(q, k, v, *, B, T, tq):
    nq = T // tq
    npair = MLA_HEADS // 2
    scale = (MLA_NOPE + MLA_ROPE) ** -0.5
    return pl.pallas_call(
        functools.partial(_mla_body, tq=tq, scale=scale),
        grid=(B, npair, nq),
        in_specs=[pl.BlockSpec((tq, 2 * HEAD_PAD), lambda b, h, i: (b * nq + i, h)),
                  pl.BlockSpec((T, 2 * HEAD_PAD), lambda b, h, i: (b, h)),
                  pl.BlockSpec((T, 2 * MLA_V), lambda b, h, i: (b, h))],
        out_specs=pl.BlockSpec((tq, 2 * MLA_V), lambda b, h, i: (b * nq + i, h)),
        out_shape=jax.ShapeDtypeStruct((B * T, MLA_HEADS * MLA_V), BF16),
        scratch_shapes=[pltpu.VMEM((2, tq, LANE), F32), pltpu.VMEM((2, tq, LANE), F32),
                        pltpu.VMEM((2, tq, LANE), F32)],
        compiler_params=_cparams(("parallel", "parallel", "parallel")),
    )(q, k, v)


def _log_sigmoid_pair(z):
    lb = jnp.minimum(z, 0.0) - jnp.log1p(jnp.exp(-jnp.abs(z)))
    return lb, lb - z


def _split_bf16(x):
    hi = x.astype(BF16)
    return hi, (x - hi.astype(F32)).astype(BF16)


def _sb_body(q_ref, k_ref, v_ref, uo_ref, o_ref, acc_ref, c_ref, *, tq):
    qi = pl.program_id(2)
    acc_ref[...] = jnp.zeros(acc_ref.shape, F32)
    c_ref[...] = jnp.zeros(c_ref.shape, F32)
    q = q_ref[...]
    lane = lax.broadcasted_iota(jnp.int32, (tq, LANE), 1)
    zero = jnp.zeros_like(q)
    qh = (jnp.where(lane < SB_HEAD_DIM, q, zero), jnp.where(lane < SB_HEAD_DIM, zero, q))
    uo = uo_ref[...]

    def block(kb, masked):
        r0 = pl.multiple_of(kb * tq, tq)
        k = k_ref[pl.ds(r0, tq), :]
        v = v_ref[pl.ds(r0, tq), :]
        for hh in range(2):
            z = _dot_nt(qh[hh], k)
            lb, lk = _log_sigmoid_pair(z)
            if masked:
                rowi = lax.broadcasted_iota(jnp.int32, z.shape, 0)
                coli = lax.broadcasted_iota(jnp.int32, z.shape, 1)
                valid = coli < rowi
                lb = jnp.where(valid, lb, -jnp.inf)
                lk = jnp.where(valid, lk, 0.0)
            hi, lo = _split_bf16(lk)
            r = _dot(hi, uo) + _dot(lo, uo)
            w = jnp.exp(lb + r[:, :tq] + c_ref[hh])
            acc_ref[hh] = acc_ref[hh] + _dot(w.astype(BF16), v)
            c_ref[hh] = c_ref[hh] + r[:, tq:]

    block(qi, True)

    def loop_body(j, carry):
        block(qi - 1 - j, False)
        return carry

    lax.fori_loop(0, qi, loop_body, 0)
    o_ref[...] = jnp.where(lane < SB_HEAD_DIM, acc_ref[0], acc_ref[1]).astype(o_ref.dtype)


def _suffix_ones(n):
    j = np.arange(n)[:, None]
    s = np.arange(n)[None, :]
    return jnp.asarray(np.concatenate([(j > s), np.ones((n, n), bool)], axis=1), BF16)


def _sb_prompt(q, k, v, *, B, T, tq):
    nq = T // tq
    npair = SB_HEADS // 2
    return pl.pallas_call(
        functools.partial(_sb_body, tq=tq),
        grid=(B, npair, nq),
        in_specs=[pl.BlockSpec((tq, LANE), lambda b, h, i: (b * nq + i, h)),
                  pl.BlockSpec((T, LANE), lambda b, h, i: (b, h)),
                  pl.BlockSpec((T, LANE), lambda b, h, i: (b, h)),
                  pl.BlockSpec((tq, 2 * tq), lambda b, h, i: (0, 0))],
        out_specs=pl.BlockSpec((tq, LANE), lambda b, h, i: (b * nq + i, h)),
        out_shape=jax.ShapeDtypeStruct((B * T, SB_WIDTH), BF16),
        scratch_shapes=[pltpu.VMEM((2, tq, LANE), F32), pltpu.VMEM((2, tq, tq), F32)],
        compiler_params=_cparams(("parallel", "parallel", "parallel")),
    )(q, k, v, _suffix_ones(tq))


def _s5_body(u_ref, h0r_ref, h0i_ref, ar_ref, ai_ref, bbr_ref, bbi_ref, cr_ref, ci_ref, d_ref, wg_ref, bg_ref,
             o_ref, sr_ref, si_ref, hr_s, hi_s, hsr, hsi, *, B, Tc):
    @pl.when(pl.program_id(0) == 0)
    def _():
        hr_s[...] = h0r_ref[...]
        hi_s[...] = h0i_ref[...]

    u = u_ref[...]
    ub = u.astype(BF16)
    for j in range(S5_NBLK):
        uj = ub[:, j * LANE:(j + 1) * LANE]
        hsr[:, j * S5_BLK_LANES:(j + 1) * S5_BLK_LANES] = _dot(uj, bbr_ref[j])
        hsi[:, j * S5_BLK_LANES:(j + 1) * S5_BLK_LANES] = _dot(uj, bbi_ref[j])
    ar, ai = ar_ref[...], ai_ref[...]

    def step(t, carry):
        hr, hi = carry
        r0 = pl.multiple_of(t * B, B)
        nr = ar * hr - ai * hi + hsr[pl.ds(r0, B), :]
        ni = ar * hi + ai * hr + hsi[pl.ds(r0, B), :]
        hsr[pl.ds(r0, B), :] = nr
        hsi[pl.ds(r0, B), :] = ni
        return nr, ni

    hr, hi = lax.fori_loop(0, Tc, step, (hr_s[...], hi_s[...]))
    hr_s[...] = hr
    hi_s[...] = hi
    sr_ref[...] = hr
    si_ref[...] = hi

    ys = []
    for j in range(S5_NBLK):
        sl = slice(j * S5_BLK_LANES, (j + 1) * S5_BLK_LANES)
        ys.append(_dot(hsr[:, sl].astype(BF16), cr_ref[j]) - _dot(hsi[:, sl].astype(BF16), ci_ref[j]))
    y = jnp.concatenate(ys, axis=1) + d_ref[...] * u
    y = jax.nn.gelu(y)
    gate = _dot(y.astype(BF16), wg_ref[...]) + bg_ref[...]
    o_ref[...] = (y * jax.nn.sigmoid(gate)).astype(o_ref.dtype)


def _s5(u, h0r, h0i, lw, *, B, T, Tc):
    rows = Tc * B
    full2 = lambda i: (0, 0)
    full3 = lambda i: (0, 0, 0)
    st = jax.ShapeDtypeStruct((B, S5_LANES), F32)
    return pl.pallas_call(
        functools.partial(_s5_body, B=B, Tc=Tc),
        grid=(T // Tc,),
        in_specs=[pl.BlockSpec((rows, S5_WIDTH), lambda i: (i, 0)),
                  pl.BlockSpec((B, S5_LANES), full2), pl.BlockSpec((B, S5_LANES), full2),
                  pl.BlockSpec((1, S5_LANES), full2), pl.BlockSpec((1, S5_LANES), full2),
                  pl.BlockSpec((S5_NBLK, LANE, S5_BLK_LANES), full3),
                  pl.BlockSpec((S5_NBLK, LANE, S5_BLK_LANES), full3),
                  pl.BlockSpec((S5_NBLK, S5_BLK_LANES, LANE), full3),
                  pl.BlockSpec((S5_NBLK, S5_BLK_LANES, LANE), full3),
                  pl.BlockSpec((1, S5_WIDTH), full2), pl.BlockSpec((S5_WIDTH, S5_WIDTH), full2),
                  pl.BlockSpec((1, S5_WIDTH), full2)],
        out_specs=(pl.BlockSpec((rows, S5_WIDTH), lambda i: (i, 0)),
                   pl.BlockSpec((B, S5_LANES), full2), pl.BlockSpec((B, S5_LANES), full2)),
        out_shape=(jax.ShapeDtypeStruct((T * B, S5_WIDTH), BF16), st, st),
        scratch_shapes=[pltpu.VMEM((B, S5_LANES), F32), pltpu.VMEM((B, S5_LANES), F32),
                        pltpu.VMEM((rows, S5_LANES), F32), pltpu.VMEM((rows, S5_LANES), F32)],
        compiler_params=_cparams(("arbitrary",)),
    )(u, h0r, h0i, lw['a_re'], lw['a_im'], lw['bb_re'], lw['bb_im'], lw['cc_re'], lw['cc_im'],
      lw['s5_d'], lw['w_glu'], lw['b_glu'])


def _out_body(x_ref, oa_ref, ob_ref, oc_ref, gt_ref, sc_ref, sh_ref, wo_ref, gffn_ref, wrh_ref, wrl_ref, br_ref,
              x1_ref, h2_ref, ti_ref, tg_ref):
    wa = MLA_HEADS * MLA_V
    o = (_dot(oa_ref[...], wo_ref[0:wa, :]) + _dot(ob_ref[...], wo_ref[wa:wa + S5_WIDTH, :])
         + _dot(oc_ref[...], wo_ref[wa + S5_WIDTH:, :]))
    x1 = x_ref[...] + gt_ref[0] * o
    x1_ref[...] = x1
    h2 = _rms(x1, gffn_ref[...]) * (1.0 + sc_ref[0]) + sh_ref[0]
    h2_ref[...] = h2.astype(BF16)
    hi, lo = _split_bf16(h2)
    logits = _dot(hi, wrh_ref[...]) + _dot(hi, wrl_ref[...]) + _dot(lo, wrh_ref[...]) + br_ref[...]
    lane = lax.broadcasted_iota(jnp.int32, logits.shape, 1)
    lanef = lane.astype(F32)
    vals, idxs = [], []
    for _ in range(TOP_K):
        m = jnp.max(logits, axis=-1, keepdims=True)
        idx = jnp.min(jnp.where(logits == m, lanef, float(ROUTER_PAD)), axis=-1, keepdims=True).astype(jnp.int32)
        vals.append(m)
        idxs.append(idx)
        logits = jnp.where(lane == idx, -jnp.inf, logits)
    es = [jnp.exp(v - vals[0]) for v in vals]
    den = es[0] + es[1] + es[2] + es[3]
    ti = jnp.zeros(lane.shape, jnp.int32)
    tg = jnp.zeros(lane.shape, F32)
    for kk in range(TOP_K):
        ti = jnp.where(lane == kk, idxs[kk], ti)
        tg = jnp.where(lane == kk, es[kk] / den, tg)
    ti_ref[...] = ti
    tg_ref[...] = tg


def _outproj(x, oa, ob, oc, gt, sc, sh, lw, *, B, T, tm):
    N = B * T
    nT = T // tm
    mt = gt.shape[1]
    row = lambda b, t: (b * nT + t, 0)
    full = lambda b, t: (0, 0)
    if mt == 1:
        mod_spec = pl.BlockSpec((1, 1, D_MODEL), lambda b, t: (b, 0, 0))
    else:
        mod_spec = pl.BlockSpec((1, tm, D_MODEL), lambda b, t: (b, t, 0))
    weights = [lw['w_out'], lw['g_ffn'], lw['wr_hi'], lw['wr_lo'], lw['b_router']]
    return pl.pallas_call(
        _out_body, grid=(B, nT),
        in_specs=[pl.BlockSpec((tm, D_MODEL), row), pl.BlockSpec((tm, MLA_HEADS * MLA_V), row),
                  pl.BlockSpec((tm, S5_WIDTH), lambda b, t: (t, b)), pl.BlockSpec((tm, SB_WIDTH), row),
                  mod_spec, mod_spec, mod_spec] + [pl.BlockSpec(w.shape, full) for w in weights],
        out_specs=(pl.BlockSpec((tm, D_MODEL), row), pl.BlockSpec((tm, D_MODEL), row),
                   pl.BlockSpec((tm, ROUTER_PAD), row), pl.BlockSpec((tm, ROUTER_PAD), row)),
        out_shape=(jax.ShapeDtypeStruct((N, D_MODEL), F32), jax.ShapeDtypeStruct((N, D_MODEL), BF16),
                   jax.ShapeDtypeStruct((N, ROUTER_PAD), jnp.int32), jax.ShapeDtypeStruct((N, ROUTER_PAD), F32)),
        compiler_params=_cparams(("parallel", "parallel")),
    )(x, oa, ob, oc, gt, sc, sh, *weights)


def _moe_body(be_ref, nu_ref, x_ref, gate_ref, wg_ref, wu_ref, wd_ref, bg_ref, bu_ref, bd_ref, o_ref):
    i = pl.program_id(0)

    @pl.when(i < nu_ref[0])
    def _():
        x = x_ref[...]
        g = jnp.minimum(_dot(x, wg_ref[0]) + bg_ref[0], SWIGLU_LIMIT)
        up = jnp.clip(_dot(x, wu_ref[0]) + bu_ref[0], -SWIGLU_LIMIT, SWIGLU_LIMIT)
        act = (up + 1.0) * (g * jax.nn.sigmoid(SWIGLU_ALPHA * g))
        y = _dot(act.astype(BF16), wd_ref[0]) + bd_ref[0]
        o_ref[...] = y * gate_ref[...]

    @pl.when(i >= nu_ref[0])
    def _():
        o_ref[...] = jnp.zeros(o_ref.shape, F32)


def _moe(block_expert, n_used, xs, slot_gate, lw, *, bm):
    n_slots = xs.shape[0]
    n_blocks = n_slots // bm
    wsel = lambda i, be, nu: (be[i], 0, 0)
    row = lambda i, be, nu: (i, 0)
    return pl.pallas_call(
        _moe_body,
        grid_spec=pltpu.PrefetchScalarGridSpec(
            num_scalar_prefetch=2, grid=(n_blocks,),
            in_specs=[pl.BlockSpec((bm, D_MODEL), row), pl.BlockSpec((bm, 1), row),
                      pl.BlockSpec((1, D_MODEL, D_FF), wsel), pl.BlockSpec((1, D_MODEL, D_FF), wsel),
                      pl.BlockSpec((1, D_FF, D_MODEL), wsel),
                      pl.BlockSpec((1, 1, D_FF), wsel), pl.BlockSpec((1, 1, D_FF), wsel),
                      pl.BlockSpec((1, 1, D_MODEL), wsel)],
            out_specs=pl.BlockSpec((bm, D_MODEL), row)),
        out_shape=jax.ShapeDtypeStruct((n_slots, D_MODEL), F32),
        compiler_params=_cparams(("arbitrary",)),
    )(block_expert, n_used, xs, slot_gate, lw['w_gate'], lw['w_up'], lw['w_down'],
      lw['b_gate'], lw['b_up'], lw['b_down'])


def _route(topi, topg, bm):
    n = topi.shape[0]
    tk = n * TOP_K
    flat_e = topi[:, :TOP_K].reshape(tk)
    flat_g = topg[:, :TOP_K].reshape(tk)
    order = jnp.argsort(flat_e).astype(jnp.int32)
    e_sorted = flat_e[order]
    counts = jnp.sum((flat_e[:, None] == jnp.arange(N_EXPERTS, dtype=jnp.int32)[None, :]).astype(jnp.int32), axis=0)
    blocks_per_e = (counts + bm - 1) // bm
    block_end = jnp.cumsum(blocks_per_e)
    group_start = jnp.cumsum(counts) - counts
    dest = ((block_end - blocks_per_e)[e_sorted] * bm + jnp.arange(tk, dtype=jnp.int32) - group_start[e_sorted])
    n_blocks = -(-tk // bm) + N_EXPERTS
    slot_tok = jnp.zeros((n_blocks * bm,), jnp.int32).at[dest].set(order // TOP_K)
    slot_gate = jnp.zeros((n_blocks * bm,), F32).at[dest].set(flat_g[order])
    pos = jnp.zeros((tk,), jnp.int32).at[order].set(dest)
    block_expert = jnp.minimum(
        jnp.searchsorted(block_end, jnp.arange(n_blocks, dtype=jnp.int32), side='right'), N_EXPERTS - 1)
    return (slot_tok, slot_gate.reshape(-1, 1), pos, block_expert.astype(jnp.int32),
            block_end[-1:].astype(jnp.int32))


def _combine_body(x_ref, y_ref, gt_ref, o_ref):
    y = y_ref[:, 0:D_MODEL]
    for kk in range(1, TOP_K):
        y = y + y_ref[:, kk * D_MODEL:(kk + 1) * D_MODEL]
    o_ref[...] = x_ref[...] + gt_ref[0] * y


def _combine(x1, y4, gt, *, B, T, tm):
    nT = T // tm
    mt = gt.shape[1]
    row = lambda b, t: (b * nT + t, 0)
    if mt == 1:
        mod_spec = pl.BlockSpec((1, 1, D_MODEL), lambda b, t: (b, 0, 0))
    else:
        mod_spec = pl.BlockSpec((1, tm, D_MODEL), lambda b, t: (b, t, 0))
    return pl.pallas_call(
        _combine_body, grid=(B, nT),
        in_specs=[pl.BlockSpec((tm, D_MODEL), row), pl.BlockSpec((tm, TOP_K * D_MODEL), row), mod_spec],
        out_specs=pl.BlockSpec((tm, D_MODEL), row),
        out_shape=jax.ShapeDtypeStruct(x1.shape, F32),
        compiler_params=_cparams(("parallel", "parallel")),
    )(x1, y4, gt)


def _page_copies(pt_ref, b, c, slot, srcs, bufs, sems, *, layer, G):
    out = []
    for g in range(G):
        page = pt_ref[b, c * G + g]
        for src, buf, sem in zip(srcs, bufs, sems):
            out.append(pltpu.make_async_copy(src.at[layer, page], buf.at[slot, pl.ds(g * PAGE_SIZE, PAGE_SIZE)],
                                             sem.at[slot]))
    return out


def _mla_dec_body(pt_ref, qm_ref, qpe_ref, q8_ref, k8_ref, cn_ref, wuk_ref, ind_ref, wuv_ref, kv_hbm, kr_hbm,
                  o_ref, kvbuf, krbuf, sem_kv, sem_kr, m_ref, l_ref, pc_ref, *, layer, G, nc, nb, scale):
    b = pl.program_id(0)
    c = pl.program_id(1)
    step = b * nc + c
    slot = lax.rem(step, 2)
    copies = functools.partial(_page_copies, pt_ref, srcs=(kv_hbm, kr_hbm), bufs=(kvbuf, krbuf),
                               sems=(sem_kv, sem_kr), layer=layer, G=G)

    @pl.when(step == 0)
    def _():
        for cp in copies(0, 0, 0):
            cp.start()

    @pl.when(step + 1 < nb * nc)
    def _():
        nxt = step + 1
        for cp in copies(nxt // nc, lax.rem(nxt, nc), 1 - slot):
            cp.start()

    @pl.when(c == 0)
    def _():
        m_ref[...] = jnp.full(m_ref.shape, -jnp.inf, F32)
        l_ref[...] = jnp.zeros(l_ref.shape, F32)
        pc_ref[...] = jnp.zeros(pc_ref.shape, F32)

    for cp in copies(b, c, slot):
        cp.wait()

    cb = kvbuf[slot].astype(BF16)
    kn = _dot(cb, wuk_ref[...])
    ss = _dot_nt(ind_ref[...], (kn * kn).astype(BF16))
    s = _dot_nt(qm_ref[0], kn.astype(BF16)) * lax.rsqrt(ss * (1.0 / MLA_NOPE) + EPS)
    s = (s + _dot_nt(qpe_ref[0], krbuf[slot].astype(BF16))) * scale
    m_prev = m_ref[...]
    m_new = jnp.maximum(m_prev, jnp.max(s, axis=-1, keepdims=True))
    alpha = jnp.exp(m_prev - m_new)
    p = jnp.exp(s - m_new[:, :1])
    l_ref[...] = alpha * l_ref[...] + jnp.sum(p, axis=-1, keepdims=True)
    pc_ref[...] = alpha * pc_ref[...] + _dot(p.astype(BF16), cb)
    m_ref[...] = m_new

    @pl.when(c == nc - 1)
    def _():
        s_new = jnp.sum(q8_ref[0].astype(F32) * k8_ref[0].astype(F32), axis=-1, keepdims=True) * scale
        m_f = jnp.maximum(m_ref[...], s_new)
        a_old = jnp.exp(m_ref[...] - m_f)
        p_new = jnp.exp(s_new - m_f)
        l_f = a_old * l_ref[...] + p_new
        ctx = (a_old * pc_ref[...] + p_new * cn_ref[0]) / l_f
        full = _dot(ctx.astype(BF16), wuv_ref[...])
        rowi = lax.broadcasted_iota(jnp.int32, full.shape, 0)
        coli = lax.broadcasted_iota(jnp.int32, full.shape, 1)
        own = (coli >= rowi * MLA_V) & (coli < (rowi + 1) * MLA_V)
        o_ref[0] = jnp.sum(jnp.where(own, full, 0.0), axis=0, keepdims=True).astype(o_ref.dtype)


def _mla_decode(page_table, qm, qpe, q8, k8, cn, lw, cache_kv, cache_kr, *, layer):
    nb, n_pages = page_table.shape
    G = min(16, n_pages)
    nc = n_pages // G
    rows = G * PAGE_SIZE
    scale = (MLA_NOPE + MLA_ROPE) ** -0.5
    per_b = lambda b, c, pt: (b, 0, 0)
    full = lambda b, c, pt: (0, 0)
    ind = jnp.asarray((np.arange(8)[:, None] == (np.arange(MLA_HEADS * MLA_NOPE)[None, :] // MLA_NOPE)), BF16)
    return pl.pallas_call(
        functools.partial(_mla_dec_body, layer=layer, G=G, nc=nc, nb=nb, scale=scale),
        grid_spec=pltpu.PrefetchScalarGridSpec(
            num_scalar_prefetch=1, grid=(nb, nc),
            in_specs=[pl.BlockSpec((1, 8, MLA_HEADS * MLA_NOPE), per_b), pl.BlockSpec((1, 8, MLA_ROPE), per_b),
                      pl.BlockSpec((1, 8, HEAD_PAD), per_b), pl.BlockSpec((1, 8, HEAD_PAD), per_b),
                      pl.BlockSpec((1, 1, MLA_KV_LORA), per_b),
                      pl.BlockSpec(lw['wuk_d'].shape, full), pl.BlockSpec(ind.shape, full),
                      pl.BlockSpec(lw['wuv'].shape, full),
                      pl.BlockSpec(memory_space=pl.ANY), pl.BlockSpec(memory_space=pl.ANY)],
            out_specs=pl.BlockSpec((1, 1, MLA_HEADS * MLA_V), per_b),
            scratch_shapes=[pltpu.VMEM((2, rows, MLA_KV_LORA), F32), pltpu.VMEM((2, rows, MLA_ROPE), F32),
                            pltpu.SemaphoreType.DMA((2,)), pltpu.SemaphoreType.DMA((2,)),
                            pltpu.VMEM((8, LANE), F32), pltpu.VMEM((8, LANE), F32),
                            pltpu.VMEM((8, MLA_KV_LORA), F32)]),
        out_shape=jax.ShapeDtypeStruct((nb, 1, MLA_HEADS * MLA_V), BF16),
        compiler_params=_cparams(("arbitrary", "arbitrary")),
    )(page_table, qm, qpe, q8, k8, cn, lw['wuk_d'], ind, lw['wuv'], cache_kv, cache_kr)


def _sb_dec_body(pt_ref, q_ref, u_ref, k_hbm, v_hbm, o_ref, kbuf, vbuf, sem_k, sem_v, acc_ref, c_ref,
                 *, layer, G, nc, nb):
    b = pl.program_id(0)
    ci = pl.program_id(1)
    step = b * nc + ci
    slot = lax.rem(step, 2)
    copies = functools.partial(_page_copies, pt_ref, srcs=(k_hbm, v_hbm), bufs=(kbuf, vbuf),
                               sems=(sem_k, sem_v), layer=layer, G=G)

    @pl.when(step == 0)
    def _():
        for cp in copies(0, nc - 1, 0):
            cp.start()

    @pl.when(step + 1 < nb * nc)
    def _():
        nxt = step + 1
        for cp in copies(nxt // nc, nc - 1 - lax.rem(nxt, nc), 1 - slot):
            cp.start()

    @pl.when(ci == 0)
    def _():
        acc_ref[...] = jnp.zeros(acc_ref.shape, F32)
        c_ref[...] = jnp.zeros(c_ref.shape, F32)

    for cp in copies(b, nc - 1 - ci, slot):
        cp.wait()

    z = _dot_nt(q_ref[0], kbuf[slot].astype(BF16))
    lb, lk = _log_sigmoid_pair(z)
    hi, lo = _split_bf16(lk)
    after = _dot(hi, u_ref[...]) + _dot(lo, u_ref[...])
    w = jnp.exp(lb + after + c_ref[...][:, :1])
    acc_ref[...] = acc_ref[...] + _dot(w.astype(BF16), vbuf[slot].astype(BF16))
    c_ref[...] = c_ref[...] + jnp.sum(lk, axis=-1, keepdims=True)

    @pl.when(ci == nc - 1)
    def _():
        acc = acc_ref[...]
        rowi = lax.broadcasted_iota(jnp.int32, acc.shape, 0)
        coli = lax.broadcasted_iota(jnp.int32, acc.shape, 1)
        own = (coli >= rowi * SB_HEAD_DIM) & (coli < (rowi + 1) * SB_HEAD_DIM)
        o_ref[0] = jnp.sum(jnp.where(own, acc, 0.0), axis=0, keepdims=True).astype(o_ref.dtype)


def _sb_decode(page_table, qs, cache_k, cache_v, *, layer):
    nb, n_pages = page_table.shape
    G = min(8, n_pages)
    nc = n_pages // G
    rows = G * PAGE_SIZE
    j = np.arange(rows)[:, None]
    s = np.arange(rows)[None, :]
    tri = jnp.asarray(j > s, BF16)
    per_b = lambda b, c, pt: (b, 0, 0)
    full = lambda b, c, pt: (0, 0)
    return pl.pallas_call(
        functools.partial(_sb_dec_body, layer=layer, G=G, nc=nc, nb=nb),
        grid_spec=pltpu.PrefetchScalarGridSpec(
            num_scalar_prefetch=1, grid=(nb, nc),
            in_specs=[pl.BlockSpec((1, 8, SB_WIDTH), per_b), pl.BlockSpec((rows, rows), full),
                      pl.BlockSpec(memory_space=pl.ANY), pl.BlockSpec(memory_space=pl.ANY)],
            out_specs=pl.BlockSpec((1, 1, SB_WIDTH), per_b),
            scratch_shapes=[pltpu.VMEM((2, rows, SB_WIDTH), F32), pltpu.VMEM((2, rows, SB_WIDTH), F32),
                            pltpu.SemaphoreType.DMA((2,)), pltpu.SemaphoreType.DMA((2,)),
                            pltpu.VMEM((8, SB_WIDTH), F32), pltpu.VMEM((8, LANE), F32)]),
        out_shape=jax.ShapeDtypeStruct((nb, 1, SB_WIDTH), BF16),
        compiler_params=_cparams(("arbitrary", "arbitrary")),
    )(page_table, qs, tri, cache_k, cache_v)


def _rope_tables(pos):
    half = MLA_ROPE // 2
    inv = ROPE_THETA ** (-jnp.arange(half, dtype=F32) / half)
    ang = pos.astype(F32)[:, None] * inv
    cos, sin = jnp.cos(ang), jnp.sin(ang)
    n = pos.shape[0]
    z16 = jnp.zeros((n, half), F32)
    z32 = jnp.zeros((n, HEAD_PAD - MLA_NOPE - MLA_ROPE), F32)
    z64 = jnp.zeros((n, MLA_NOPE), F32)
    cos_t = jnp.concatenate([jnp.ones((n, MLA_NOPE), F32), cos, cos, z32], axis=1)
    sin_a = jnp.concatenate([z64, -sin, z16, z32], axis=1)
    sin_b = jnp.concatenate([z64, z16, sin, z32], axis=1)
    return cos_t, sin_a, sin_b


def _pad_heads(w, width):
    K = w.shape[0]
    w = w.reshape(K, MLA_HEADS, width)
    w = jnp.pad(w, ((0, 0), (0, 0), (0, HEAD_PAD - width)))
    return w.reshape(K, MLA_HEADS * HEAD_PAD)


def _block_diag(w, rows_first):
    a, b = w.shape[1], w.shape[2]
    w = w.reshape(S5_NBLK, S5_BLK_GROUPS, a, b)
    eye = jnp.eye(S5_BLK_GROUPS, dtype=w.dtype)
    out = jnp.einsum('jgab,gh->jgahb', w, eye)
    return out.reshape(S5_NBLK, S5_BLK_GROUPS * a, S5_BLK_GROUPS * b)


def _layer_weights(l, p, disc):
    a_re, a_im, f_re, f_im = [d[l] for d in disc]
    lw = {}
    w_in = p['w_in'][l]
    kpe_cols = jnp.pad(w_in[:, 384:416], ((0, 0), (MLA_NOPE, HEAD_PAD - MLA_NOPE - MLA_ROPE)))
    lw['w1'] = jnp.concatenate([w_in[:, 0:384], kpe_cols, w_in[:, 416:]], axis=1).astype(BF16)
    row = lambda v: v.reshape(1, -1).astype(F32)
    lw['g_mix'] = row(p['g_mix'][l])
    lw['g_q_lat'] = row(p['g_q_lat'][l])
    lw['g_kv_lat'] = row(p['g_kv_lat'][l])
    lw['wuq'] = _pad_heads(p['w_uq'][l], MLA_NOPE + MLA_ROPE).astype(BF16)
    lw['wuk'] = _pad_heads(p['w_uk'][l], MLA_NOPE).astype(BF16)
    lw['wuk_d'] = p['w_uk'][l].astype(BF16)
    lw['wuv'] = p['w_uv'][l].astype(BF16)
    zpad = jnp.zeros((HEAD_PAD - MLA_NOPE - MLA_ROPE,), F32)
    lw['gq'] = row(jnp.concatenate([p['g_qn_nope'][l], p['g_qn_rope'][l], zpad]))
    lw['gk'] = row(jnp.concatenate([p['g_kn_nope'][l], jnp.zeros((HEAD_PAD - MLA_NOPE,), F32)]))
    lw['gkr'] = row(jnp.concatenate([jnp.zeros((MLA_NOPE,), F32), p['g_kn_rope'][l], zpad]))
    br, bi = p['s5_b_re'][l], p['s5_b_im'][l]
    bb_re = f_re[..., None] * br - f_im[..., None] * bi
    bb_im = f_re[..., None] * bi + f_im[..., None] * br
    lw['bb_re'] = _block_diag(jnp.swapaxes(bb_re, 1, 2), True).astype(BF16)
    lw['bb_im'] = _block_diag(jnp.swapaxes(bb_im, 1, 2), True).astype(BF16)
    lw['cc_re'] = _block_diag(jnp.swapaxes(p['s5_c_re'][l], 1, 2), False).astype(BF16)
    lw['cc_im'] = _block_diag(jnp.swapaxes(p['s5_c_im'][l], 1, 2), False).astype(BF16)
    lw['a_re'] = a_re.reshape(1, S5_LANES)
    lw['a_im'] = a_im.reshape(1, S5_LANES)
    lw['s5_d'] = row(p['s5_d'][l])
    lw['w_glu'] = p['w_glu'][l].astype(BF16)
    lw['b_glu'] = row(p['b_glu'][l])
    lw['w_out'] = p['w_out'][l].astype(BF16)
    lw['g_ffn'] = row(p['g_ffn'][l])
    wr = jnp.pad(p['w_router'][l], ((0, 0), (0, ROUTER_PAD - N_EXPERTS)))
    lw['wr_hi'] = wr.astype(BF16)
    lw['wr_lo'] = (wr - lw['wr_hi'].astype(F32)).astype(BF16)
    lw['b_router'] = row(jnp.concatenate([p['b_router'][l], jnp.full((ROUTER_PAD - N_EXPERTS,), NEG_BIG, F32)]))
    w_gu = p['w_gu'][l].reshape(N_EXPERTS, D_MODEL, D_FF, 2)
    lw['w_gate'] = w_gu[..., 0].astype(BF16)
    lw['w_up'] = w_gu[..., 1].astype(BF16)
    lw['w_down'] = p['w_down'][l].astype(BF16)
    b_gu = p['b_gu'][l].reshape(N_EXPERTS, 1, D_FF, 2)
    lw['b_gate'] = b_gu[..., 0]
    lw['b_up'] = b_gu[..., 1]
    lw['b_down'] = p['b_down'][l].reshape(N_EXPERTS, 1, D_MODEL)
    return lw


def _split_mod(mod):
    return [mod[..., i * D_MODEL:(i + 1) * D_MODEL] for i in range(N_MOD)]


def _ffn(x1, h2, topi, topg, gt_f, lw, *, B, T, tm, bm):
    n = B * T
    slot_tok, slot_gate, pos, block_expert, n_used = _route(topi, topg, bm)
    xs = jnp.take(h2, slot_tok, axis=0)
    ys = _moe(block_expert, n_used, xs, slot_gate, lw, bm=bm)
    y4 = jnp.take(ys, pos, axis=0).reshape(n, TOP_K * D_MODEL)
    return _combine(x1, y4, gt_f, B=B, T=T, tm=tm)


def _prompt_layer(x, mods, tabs, lw, *, B, T):
    sh_a, sc_a, gt_a, sh_f, sc_f, gt_f = mods
    tm = min(256, T)
    q, k, v, ckv, kpe, u, sq, sk16, sv16, sk, sv = _proj(x, sc_a, sh_a, tabs, lw, B=B, T=T, tm=tm, qk_dtype=BF16)
    tq = min(256, T)
    o_a = _mla_prompt(q, k, v, B=B, T=T, tq=tq)
    o_c = _sb_prompt(sq, sk16, sv16, B=B, T=T, tq=tq)
    zeros = jnp.zeros((B, S5_LANES), F32)
    Tc = min(64, T)
    o_b, s_re, s_im = _s5(u.reshape(T * B, S5_WIDTH), zeros, zeros, lw, B=B, T=T, Tc=Tc)
    x1, h2, topi, topg = _outproj(x, o_a, o_b.reshape(T, B * S5_WIDTH), o_c, gt_a, sc_f, sh_f, lw, B=B, T=T, tm=tm)
    x2 = _ffn(x1, h2, topi, topg, gt_f, lw, B=B, T=T, tm=tm, bm=min(256, max(16, (B * T * TOP_K) // N_EXPERTS)))
    state = (ckv.reshape(B, T, MLA_KV_LORA), kpe.reshape(B, T, MLA_ROPE),
             sk.reshape(B, T, SB_HEADS, SB_HEAD_DIM), sv.reshape(B, T, SB_HEADS, SB_HEAD_DIM),
             s_re.reshape(B, S5_GROUPS, S5_STATE), s_im.reshape(B, S5_GROUPS, S5_STATE))
    return x2, state


def _sample_layer(x, mods, tabs, lw, caches, h0, page_table, p, *, l, DB):
    sh_a, sc_a, gt_a, sh_f, sc_f, gt_f = mods
    cache_kv, cache_kr, cache_sk, cache_sv = caches
    q, k, v, ckv, kpe, u, sq, sk16, sv16, sk, sv = _proj(x, sc_a, sh_a, tabs, lw, B=1, T=DB, tm=DB, qk_dtype=F32)
    q8 = jnp.pad(q.reshape(DB, MLA_HEADS, HEAD_PAD), ((0, 0), (0, 8 - MLA_HEADS), (0, 0)))
    k8 = jnp.pad(k.reshape(DB, MLA_HEADS, HEAD_PAD), ((0, 0), (0, 8 - MLA_HEADS), (0, 0)))
    qg = q8[:, :, :MLA_NOPE] * p['g_kn_nope'][l][None, None, :]
    eye = jnp.eye(8, MLA_HEADS, dtype=F32)
    qm = jnp.einsum('bhd,hg->bhgd', qg, eye).reshape(DB, 8, MLA_HEADS * MLA_NOPE).astype(BF16)
    qpe = q8[:, :, MLA_NOPE:MLA_NOPE + MLA_ROPE].astype(BF16)
    o_a = _mla_decode(page_table, qm, qpe, q8.astype(BF16), k8.astype(BF16), ckv.reshape(DB, 1, MLA_KV_LORA),
                      lw, cache_kv, cache_kr, layer=l).reshape(DB, MLA_HEADS * MLA_V)
    sq4 = sq.reshape(DB, SB_HEADS, SB_HEAD_DIM)
    eye4 = jnp.eye(8, SB_HEADS, dtype=BF16)
    qs = jnp.einsum('bhd,rh,hg->brgd', sq4, eye4, jnp.eye(SB_HEADS, dtype=BF16)).reshape(DB, 8, SB_WIDTH)
    o_c = _sb_decode(page_table, qs, cache_sk, cache_sv, layer=l).reshape(DB, SB_WIDTH)
    o_b, s_re, s_im = _s5(u, h0[0], h0[1], lw, B=DB, T=1, Tc=1)
    x1, h2, topi, topg = _outproj(x, o_a, o_b, o_c, gt_a, sc_f, sh_f, lw, B=1, T=DB, tm=DB)
    x2 = _ffn(x1, h2, topi, topg, gt_f, lw, B=1, T=DB, tm=DB, bm=16)
    state = (ckv.reshape(DB, 1, MLA_KV_LORA), kpe.reshape(DB, 1, MLA_ROPE),
             sk.reshape(DB, 1, SB_HEADS, SB_HEAD_DIM), sv.reshape(DB, 1, SB_HEADS, SB_HEAD_DIM),
             s_re.reshape(DB, S5_GROUPS, S5_STATE), s_im.reshape(DB, S5_GROUPS, S5_STATE))
    return x2, state


def kernel(x_prompt, x_sample, cache_kv_latent, cache_k_rope, cache_sb_k, cache_sb_v, state_s5_re, state_s5_im, page_table, c_prompt, c_sample, w_ada, b_ada, g_mix, w_in, g_q_lat, w_uq, g_qn_nope, g_qn_rope, g_kv_lat, g_kn_rope, w_uk, w_uv, g_kn_nope, s5_lam_re, s5_lam_im, s5_log_step, s5_b_re, s5_b_im, s5_c_re, s5_c_im, s5_d, w_glu, b_glu, w_out, g_ffn, w_router, b_router, w_gu, b_gu, w_down, b_down):
    B, T, _ = x_prompt.shape
    DB, DT, _ = x_sample.shape
    assert DT == 1, "the sample group carries one new token per row"
    L = w_ada.shape[0]
    n_pool = cache_kv_latent.shape[1]
    past_len = page_table.shape[1] * PAGE_SIZE
    p = dict(w_in=w_in, g_mix=g_mix, g_q_lat=g_q_lat, w_uq=w_uq, g_qn_nope=g_qn_nope, g_qn_rope=g_qn_rope,
             g_kv_lat=g_kv_lat, g_kn_rope=g_kn_rope, w_uk=w_uk, w_uv=w_uv, g_kn_nope=g_kn_nope,
             s5_b_re=s5_b_re, s5_b_im=s5_b_im, s5_c_re=s5_c_re, s5_c_im=s5_c_im, s5_d=s5_d, w_glu=w_glu,
             b_glu=b_glu, w_out=w_out, g_ffn=g_ffn, w_router=w_router, b_router=b_router, w_gu=w_gu, b_gu=b_gu,
             w_down=w_down, b_down=b_down)
    mod = _ada(jnp.concatenate([c_prompt, c_sample], axis=0), w_ada, b_ada)
    disc = _s5disc(s5_lam_re, s5_lam_im, s5_log_step)
    tabs_p = _rope_tables(jnp.arange(T, dtype=jnp.int32))
    tabs_s = _rope_tables(jnp.full((DB,), past_len, jnp.int32))
    caches = (cache_kv_latent, cache_k_rope,
              cache_sb_k.reshape(L, n_pool, PAGE_SIZE, SB_WIDTH), cache_sb_v.reshape(L, n_pool, PAGE_SIZE, SB_WIDTH))
    xp = x_prompt.reshape(B * T, D_MODEL)
    xs = x_sample.reshape(DB, D_MODEL)
    p_rows, s_rows = [], []
    for l in range(L):
        lw = _layer_weights(l, p, disc)
        mods_p = _split_mod(mod[l, :B].reshape(B, 1, N_MOD * D_MODEL))
        mods_s = _split_mod(mod[l, B:].reshape(1, DB, N_MOD * D_MODEL))
        xp, st_p = _prompt_layer(xp, mods_p, tabs_p, lw, B=B, T=T)
        p_rows.append(st_p)
        h0 = (state_s5_re[l].reshape(DB, S5_LANES), state_s5_im[l].reshape(DB, S5_LANES))
        xs, st_s = _sample_layer(xs, mods_s, tabs_s, lw, caches, h0, page_table, p, l=l, DB=DB)
        s_rows.append(st_s)
    p_out = [jnp.stack(a) for a in zip(*p_rows)]
    s_out = [jnp.stack(a) for a in zip(*s_rows)]
    return (xp.reshape(B, T, D_MODEL), xs.reshape(DB, 1, D_MODEL), *p_out, *s_out)
```

```python
import functools

import jax
import jax.numpy as jnp
import numpy as np
from jax import lax
from jax.experimental import pallas as pl
from jax.experimental.pallas import tpu as pltpu
from jax.experimental.pallas import tpu_sc as plsc

F32 = jnp.float32
BF16 = jnp.bfloat16

D_MODEL = 1024
PAGE_SIZE = 128
MLA_HEADS = 6
MLA_NOPE = 64
MLA_ROPE = 32
MLA_V = 64
MLA_Q_LORA = 256
MLA_KV_LORA = 128
ROPE_THETA = 10000.0
S5_CH = 16
S5_GROUPS = 24
S5_STATE = 64
S5_WIDTH = S5_GROUPS * S5_CH
S5_LANES = S5_GROUPS * S5_STATE
SB_HEADS = 4
SB_HEAD_DIM = 64
SB_WIDTH = SB_HEADS * SB_HEAD_DIM
N_EXPERTS = 32
TOP_K = 4
D_FF = 1024
SWIGLU_LIMIT = 7.0
SWIGLU_ALPHA = 1.702
N_MOD = 6
EPS = 1e-6

LANE = 128
HEAD_PAD = 128
S5_BLK_GROUPS = LANE // S5_CH
S5_NBLK = S5_GROUPS // S5_BLK_GROUPS
S5_BLK_LANES = S5_BLK_GROUPS * S5_STATE
ROUTER_PAD = 128
NEG_BIG = -1e30
VMEM_LIMIT = 56 * 1024 * 1024

_C_Q, _C_KV, _C_KPE, _C_U, _C_SQ, _C_SK, _C_SV, _C_END = 0, 256, 384, 512, 896, 1152, 1408, 1664


def _cparams(sem):
    return pltpu.CompilerParams(dimension_semantics=sem, vmem_limit_bytes=VMEM_LIMIT)


def _dot(a, b):
    return jnp.dot(a, b, preferred_element_type=F32)


def _dot_nt(a, b):
    return lax.dot_general(a, b, (((1,), (1,)), ((), ())), preferred_element_type=F32)


def _rms(x, g):
    return x * lax.rsqrt(jnp.mean(x * x, axis=-1, keepdims=True) + EPS) * g


def _rope_lanes(x, cos, sin_a, sin_b):
    return x * cos + pltpu.roll(x, HEAD_PAD - MLA_ROPE // 2, 1) * sin_a + pltpu.roll(x, MLA_ROPE // 2, 1) * sin_b


def _ada_body(c_ref, w_ref, b_ref, o_ref):
    c = c_ref[...]
    s = (c * jax.nn.sigmoid(c)).astype(BF16)
    o_ref[0] = _dot(s, w_ref[0].astype(BF16)) + b_ref[0]


def _ada(c_all, w_ada, b_ada):
    L = w_ada.shape[0]
    R = c_all.shape[0]
    ncol = N_MOD * D_MODEL // D_MODEL
    return pl.pallas_call(
        _ada_body,
        grid=(L, ncol),
        in_specs=[pl.BlockSpec((R, D_MODEL), lambda l, j: (0, 0)),
                  pl.BlockSpec((1, D_MODEL, D_MODEL), lambda l, j: (l, 0, j)),
                  pl.BlockSpec((1, 1, D_MODEL), lambda l, j: (l, 0, j))],
        out_specs=pl.BlockSpec((1, R, D_MODEL), lambda l, j: (l, 0, j)),
        out_shape=jax.ShapeDtypeStruct((L, R, N_MOD * D_MODEL), F32),
        compiler_params=_cparams(("parallel", "parallel")),
    )(c_all, w_ada, b_ada.reshape(L, 1, N_MOD * D_MODEL))


def _s5disc_body(lr_ref, li_ref, ls_ref, ar_ref, ai_ref, fr_ref, fi_ref):
    lr, li = lr_ref[...], li_ref[...]
    dt = jnp.exp(ls_ref[...])
    mag = jnp.exp(lr * dt)
    a_re, a_im = mag * jnp.cos(li * dt), mag * jnp.sin(li * dt)
    nr, ni = a_re - 1.0, a_im
    den = lr * lr + li * li
    ar_ref[...] = a_re
    ai_ref[...] = a_im
    fr_ref[...] = (nr * lr + ni * li) / den
    fi_ref[...] = (ni * lr - nr * li) / den


def _s5disc(lam_re, lam_im, log_step):
    L, G, N = lam_re.shape
    R = L * G
    sds = jax.ShapeDtypeStruct((R, N), F32)
    outs = pl.pallas_call(_s5disc_body, out_shape=(sds, sds, sds, sds))(
        lam_re.reshape(R, N), lam_im.reshape(R, N), log_step.reshape(R, 1))
    return [o.reshape(L, G, N) for o in outs]


def _proj_body(x_ref, sc_ref, sh_ref, gmix_ref, w1_ref, gql_ref, wuq_ref, gkv_ref, wuk_ref, wuv_ref,
               gq_ref, gk_ref, gkr_ref, cos_ref, sa_ref, sb_ref,
               q_ref, k_ref, v_ref, ckv_ref, kpe_ref, u_ref, sq_ref, sk16_ref, sv16_ref, sk_ref, sv_ref):
    x = x_ref[...]
    hn = _rms(x, gmix_ref[...]) * (1.0 + sc_ref[0]) + sh_ref[0]
    z = _dot(hn.astype(BF16), w1_ref[...])
    tm = x.shape[0]
    lane = lax.broadcasted_iota(jnp.int32, (tm, HEAD_PAD), 1)
    nope = lane < MLA_NOPE
    cos, sin_a, sin_b = cos_ref[...], sa_ref[...], sb_ref[...]

    ckv = _rms(z[:, _C_KV:_C_KPE], gkv_ref[...])
    ckv_ref[...] = ckv
    kraw = z[:, _C_KPE:_C_U]
    kr = kraw * lax.rsqrt(jnp.sum(kraw * kraw, axis=-1, keepdims=True) * (1.0 / MLA_ROPE) + EPS) * gkr_ref[...]
    kr = _rope_lanes(kr, cos, sin_a, sin_b)
    kpe_ref[...] = pltpu.roll(kr, HEAD_PAD - MLA_NOPE, 1)[:, :MLA_ROPE]

    qn = _rms(z[:, _C_Q:_C_KV], gql_ref[...]).astype(BF16)
    qa = _dot(qn, wuq_ref[...])
    cb = ckv.astype(BF16)
    ka = _dot(cb, wuk_ref[...])
    for h in range(MLA_HEADS):
        sl = slice(h * HEAD_PAD, (h + 1) * HEAD_PAD)
        qh = qa[:, sl]
        q2 = qh * qh
        ss_n = jnp.sum(jnp.where(nope, q2, 0.0), axis=-1, keepdims=True) * (1.0 / MLA_NOPE)
        ss_r = jnp.sum(jnp.where(nope, 0.0, q2), axis=-1, keepdims=True) * (1.0 / MLA_ROPE)
        qh = qh * jnp.where(nope, lax.rsqrt(ss_n + EPS), lax.rsqrt(ss_r + EPS)) * gq_ref[...]
        q_ref[:, sl] = _rope_lanes(qh, cos, sin_a, sin_b).astype(q_ref.dtype)
        kh = ka[:, sl]
        kh = kh * lax.rsqrt(jnp.sum(kh * kh, axis=-1, keepdims=True) * (1.0 / MLA_NOPE) + EPS) * gk_ref[...]
        k_ref[:, sl] = (kh + kr).astype(k_ref.dtype)
    v_ref[...] = _dot(cb, wuv_ref[...]).astype(v_ref.dtype)

    u_ref[...] = z[:, _C_U:_C_SQ]
    sq_ref[...] = (z[:, _C_SQ:_C_SK] * (SB_HEAD_DIM ** -0.5)).astype(BF16)
    sk = z[:, _C_SK:_C_SV]
    sv = z[:, _C_SV:_C_END]
    sk_ref[...] = sk
    sv_ref[...] = sv
    sk16_ref[...] = sk.astype(BF16)
    sv16_ref[...] = sv.astype(BF16)


def _proj(x, sc, sh, tabs, lw, *, B, T, tm, qk_dtype):
    N = B * T
    nT = T // tm
    mt = sc.shape[1]
    row = lambda b, t: (b * nT + t, 0)
    full = lambda b, t: (0, 0)
    if mt == 1:
        mod_spec = pl.BlockSpec((1, 1, D_MODEL), lambda b, t: (b, 0, 0))
    else:
        mod_spec = pl.BlockSpec((1, tm, D_MODEL), lambda b, t: (b, t, 0))
    tab_spec = pl.BlockSpec((tm, HEAD_PAD), lambda b, t: (t, 0))

    def wspec(a):
        return pl.BlockSpec(a.shape, full)

    weights = [lw['g_mix'], lw['w1'], lw['g_q_lat'], lw['wuq'], lw['g_kv_lat'], lw['wuk'], lw['wuv'],
               lw['gq'], lw['gk'], lw['gkr']]
    in_specs = ([pl.BlockSpec((tm, D_MODEL), row), mod_spec, mod_spec] + [wspec(w) for w in weights[:1]]
                + [wspec(w) for w in weights[1:]] + [tab_spec] * 3)
    QW = MLA_HEADS * HEAD_PAD
    VW = MLA_HEADS * MLA_V
    out_shape = (
        jax.ShapeDtypeStruct((N, QW), qk_dtype), jax.ShapeDtypeStruct((N, QW), qk_dtype),
        jax.ShapeDtypeStruct((N, VW), BF16),
        jax.ShapeDtypeStruct((N, MLA_KV_LORA), F32), jax.ShapeDtypeStruct((N, MLA_ROPE), F32),
        jax.ShapeDtypeStruct((T, B * S5_WIDTH), F32),
        jax.ShapeDtypeStruct((N, SB_WIDTH), BF16), jax.ShapeDtypeStruct((N, SB_WIDTH), BF16),
        jax.ShapeDtypeStruct((N, SB_WIDTH), BF16),
        jax.ShapeDtypeStruct((N, SB_WIDTH), F32), jax.ShapeDtypeStruct((N, SB_WIDTH), F32))
    out_specs = (
        pl.BlockSpec((tm, QW), row), pl.BlockSpec((tm, QW), row), pl.BlockSpec((tm, VW), row),
        pl.BlockSpec((tm, MLA_KV_LORA), row), pl.BlockSpec((tm, MLA_ROPE), row),
        pl.BlockSpec((tm, S5_WIDTH), lambda b, t: (t, b)),
        pl.BlockSpec((tm, SB_WIDTH), row), pl.BlockSpec((tm, SB_WIDTH), row), pl.BlockSpec((tm, SB_WIDTH), row),
        pl.BlockSpec((tm, SB_WIDTH), row), pl.BlockSpec((tm, SB_WIDTH), row))
    return pl.pallas_call(
        _proj_body, grid=(B, nT), in_specs=in_specs, out_specs=out_specs, out_shape=out_shape,
        compiler_params=_cparams(("parallel", "parallel")),
    )(x, sc, sh, *weights, *tabs)


def _mla_body(q_ref, k_ref, v_ref, o_ref, m_ref, l_ref, acc_ref, *, tq, scale):
    qi = pl.program_id(2)
    m_ref[...] = jnp.full(m_ref.shape, -jnp.inf, F32)
    l_ref[...] = jnp.zeros(l_ref.shape, F32)
    acc_ref[...] = jnp.zeros(acc_ref.shape, F32)
    q = q_ref[...]

    def block(kb, masked):
        r0 = pl.multiple_of(kb * tq, tq)
        k = k_ref[pl.ds(r0, tq), :]
        v = v_ref[pl.ds(r0, tq), :]
        for hh in range(2):
            sl = slice(hh * HEAD_PAD, (hh + 1) * HEAD_PAD)
            s = _dot_nt(q[:, sl], k[:, sl]) * scale
            if masked:
                rowi = lax.broadcasted_iota(jnp.int32, s.shape, 0)
                coli = lax.broadcasted_iota(jnp.int32, s.shape, 1)
                s = jnp.where(coli <= rowi, s, -jnp.inf)
            m_prev = m_ref[hh]
            m_new = jnp.maximum(m_prev, jnp.max(s, axis=-1, keepdims=True))
            alpha = jnp.exp(m_prev - m_new)
            p = jnp.exp(s - m_new[:, :1])
            l_ref[hh] = alpha * l_ref[hh] + jnp.sum(p, axis=-1, keepdims=True)
            acc_ref[hh] = alpha * acc_ref[hh] + _dot(p.astype(BF16), v)
            m_ref[hh] = m_new

    def loop_body(kb, carry):
        block(kb, False)
        return carry

    lax.fori_loop(0, qi, loop_body, 0)
    block(qi, True)
    lane = lax.broadcasted_iota(jnp.int32, (tq, LANE), 1)
    o_ref[...] = jnp.where(lane < MLA_V, acc_ref[0] / l_ref[0], acc_ref[1] / l_ref[1]).astype(o_ref.dtype)


def _mla_prompt(q, k, v, *, B, T, tq):
    nq = T // tq
    npair = MLA_HEADS // 2
    scale = (MLA_NOPE + MLA_ROPE) ** -0.5
    return pl.pallas_call(
        functools.partial(_mla_body, tq=tq, scale=scale),
        grid=(B, npair, nq),
        in_specs=[pl.BlockSpec((tq, 2 * HEAD_PAD), lambda b, h, i: (b * nq + i, h)),
                  pl.BlockSpec((T, 2 * HEAD_PAD), lambda b, h, i: (b, h)),
                  pl.BlockSpec((T, 2 * MLA_V), lambda b, h, i: (b, h))],
        out_specs=pl.BlockSpec((tq, 2 * MLA_V), lambda b, h, i: (b * nq + i, h)),
        out_shape=jax.ShapeDtypeStruct((B * T, MLA_HEADS * MLA_V), BF16),
        scratch_shapes=[pltpu.VMEM((2, tq, LANE), F32), pltpu.VMEM((2, tq, LANE), F32),
                        pltpu.VMEM((2, tq, LANE), F32)],
        compiler_params=_cparams(("parallel", "parallel", "parallel")),
    )(q, k, v)


def _log_sigmoid_pair(z):
    lb = jnp.minimum(z, 0.0) - jnp.log1p(jnp.exp(-jnp.abs(z)))
    return lb, lb - z


def _split_bf16(x):
    hi = x.astype(BF16)
    return hi, (x - hi.astype(F32)).astype(BF16)


def _sb_body(q_ref, k_ref, v_ref, uo_ref, o_ref, acc_ref, c_ref, *, tq):
    qi = pl.program_id(1)
    acc_ref[...] = jnp.zeros(acc_ref.shape, F32)
    c_ref[...] = jnp.zeros(c_ref.shape, F32)
    lane = lax.broadcasted_iota(jnp.int32, (tq, LANE), 1)
    qh = []
    for pr in range(SB_HEADS // 2):
        q = q_ref[:, pr * LANE:(pr + 1) * LANE]
        zero = jnp.zeros_like(q)
        qh += [jnp.where(lane < SB_HEAD_DIM, q, zero), jnp.where(lane < SB_HEAD_DIM, zero, q)]
    uo = uo_ref[...]

    def block(kb, masked):
        r0 = pl.multiple_of(kb * tq, tq)
        for h in range(SB_HEADS):
            pr = h // 2
            k = k_ref[pl.ds(r0, tq), pr * LANE:(pr + 1) * LANE]
            v = v_ref[pl.ds(r0, tq), pr * LANE:(pr + 1) * LANE]
            z = _dot_nt(qh[h], k)
            lb, lk = _log_sigmoid_pair(z)
            if masked:
                rowi = lax.broadcasted_iota(jnp.int32, z.shape, 0)
                coli = lax.broadcasted_iota(jnp.int32, z.shape, 1)
                valid = coli < rowi
                lb = jnp.where(valid, lb, -jnp.inf)
                lk = jnp.where(valid, lk, 0.0)
            hi, lo = _split_bf16(lk)
            r = _dot(hi, uo) + _dot(lo, uo)
            w = jnp.exp(lb + r[:, :tq] + c_ref[h])
            acc_ref[h] = acc_ref[h] + _dot(w.astype(BF16), v)
            c_ref[h] = c_ref[h] + r[:, tq:]

    block(qi, True)

    def loop_body(j, carry):
        block(qi - 1 - j, False)
        return carry

    lax.fori_loop(0, qi, loop_body, 0)
    for pr in range(SB_HEADS // 2):
        o_ref[:, pr * LANE:(pr + 1) * LANE] = jnp.where(
            lane < SB_HEAD_DIM, acc_ref[2 * pr], acc_ref[2 * pr + 1]).astype(o_ref.dtype)


def _suffix_ones(n):
    j = np.arange(n)[:, None]
    s = np.arange(n)[None, :]
    return jnp.asarray(np.concatenate([(j > s), np.ones((n, n), bool)], axis=1), BF16)


def _sb_prompt(q, k, v, *, B, T, tq):
    nq = T // tq
    return pl.pallas_call(
        functools.partial(_sb_body, tq=tq),
        grid=(B, nq),
        in_specs=[pl.BlockSpec((tq, SB_WIDTH), lambda b, i: (b * nq + i, 0)),
                  pl.BlockSpec((T, SB_WIDTH), lambda b, i: (b, 0)),
                  pl.BlockSpec((T, SB_WIDTH), lambda b, i: (b, 0)),
                  pl.BlockSpec((tq, 2 * tq), lambda b, i: (0, 0))],
        out_specs=pl.BlockSpec((tq, SB_WIDTH), lambda b, i: (b * nq + i, 0)),
        out_shape=jax.ShapeDtypeStruct((B * T, SB_WIDTH), BF16),
        scratch_shapes=[pltpu.VMEM((SB_HEADS, tq, LANE), F32), pltpu.VMEM((SB_HEADS, tq, tq), F32)],
        compiler_params=_cparams(("parallel", "parallel")),
    )(q, k, v, _suffix_ones(tq))


def _s5_body(u_ref, h0r_ref, h0i_ref, ar_ref, ai_ref, bbr_ref, bbi_ref, cr_ref, ci_ref, d_ref, wg_ref, bg_ref,
             o_ref, sr_ref, si_ref, hr_s, hi_s, hsr, hsi, *, B, Tc):
    @pl.when(pl.program_id(0) == 0)
    def _():
        hr_s[...] = h0r_ref[...]
        hi_s[...] = h0i_ref[...]

    u = u_ref[...]
    ub = u.astype(BF16)
    for j in range(S5_NBLK):
        uj = ub[:, j * LANE:(j + 1) * LANE]
        hsr[:, j * S5_BLK_LANES:(j + 1) * S5_BLK_LANES] = _dot(uj, bbr_ref[j])
        hsi[:, j * S5_BLK_LANES:(j + 1) * S5_BLK_LANES] = _dot(uj, bbi_ref[j])
    ar, ai = ar_ref[...], ai_ref[...]

    def step(t, carry):
        hr, hi = carry
        r0 = pl.multiple_of(t * B, B)
        nr = ar * hr - ai * hi + hsr[pl.ds(r0, B), :]
        ni = ar * hi + ai * hr + hsi[pl.ds(r0, B), :]
        hsr[pl.ds(r0, B), :] = nr
        hsi[pl.ds(r0, B), :] = ni
        return nr, ni

    hr, hi = lax.fori_loop(0, Tc, step, (hr_s[...], hi_s[...]))
    hr_s[...] = hr
    hi_s[...] = hi
    sr_ref[...] = hr
    si_ref[...] = hi

    ys = []
    for j in range(S5_NBLK):
        sl = slice(j * S5_BLK_LANES, (j + 1) * S5_BLK_LANES)
        ys.append(_dot(hsr[:, sl].astype(BF16), cr_ref[j]) - _dot(hsi[:, sl].astype(BF16), ci_ref[j]))
    y = jnp.concatenate(ys, axis=1) + d_ref[...] * u
    y = jax.nn.gelu(y)
    gate = _dot(y.astype(BF16), wg_ref[...]) + bg_ref[...]
    o_ref[...] = (y * jax.nn.sigmoid(gate)).astype(o_ref.dtype)


def _s5(u, h0r, h0i, lw, *, B, T, Tc):
    rows = Tc * B
    full2 = lambda i: (0, 0)
    full3 = lambda i: (0, 0, 0)
    st = jax.ShapeDtypeStruct((B, S5_LANES), F32)
    return pl.pallas_call(
        functools.partial(_s5_body, B=B, Tc=Tc),
        grid=(T // Tc,),
        in_specs=[pl.BlockSpec((rows, S5_WIDTH), lambda i: (i, 0)),
                  pl.BlockSpec((B, S5_LANES), full2), pl.BlockSpec((B, S5_LANES), full2),
                  pl.BlockSpec((1, S5_LANES), full2), pl.BlockSpec((1, S5_LANES), full2),
                  pl.BlockSpec((S5_NBLK, LANE, S5_BLK_LANES), full3),
                  pl.BlockSpec((S5_NBLK, LANE, S5_BLK_LANES), full3),
                  pl.BlockSpec((S5_NBLK, S5_BLK_LANES, LANE), full3),
                  pl.BlockSpec((S5_NBLK, S5_BLK_LANES, LANE), full3),
                  pl.BlockSpec((1, S5_WIDTH), full2), pl.BlockSpec((S5_WIDTH, S5_WIDTH), full2),
                  pl.BlockSpec((1, S5_WIDTH), full2)],
        out_specs=(pl.BlockSpec((rows, S5_WIDTH), lambda i: (i, 0)),
                   pl.BlockSpec((B, S5_LANES), full2), pl.BlockSpec((B, S5_LANES), full2)),
        out_shape=(jax.ShapeDtypeStruct((T * B, S5_WIDTH), BF16), st, st),
        scratch_shapes=[pltpu.VMEM((B, S5_LANES), F32), pltpu.VMEM((B, S5_LANES), F32),
                        pltpu.VMEM((rows, S5_LANES), F32), pltpu.VMEM((rows, S5_LANES), F32)],
        compiler_params=_cparams(("arbitrary",)),
    )(u, h0r, h0i, lw['a_re'], lw['a_im'], lw['bb_re'], lw['bb_im'], lw['cc_re'], lw['cc_im'],
      lw['s5_d'], lw['w_glu'], lw['b_glu'])


def _out_body(x_ref, oa_ref, ob_ref, oc_ref, gt_ref, sc_ref, sh_ref, wo_ref, gffn_ref, wrh_ref, wrl_ref, br_ref,
              x1_ref, h2_ref, ti_ref, tg_ref):
    wa = MLA_HEADS * MLA_V
    o = (_dot(oa_ref[...], wo_ref[0:wa, :]) + _dot(ob_ref[...], wo_ref[wa:wa + S5_WIDTH, :])
         + _dot(oc_ref[...], wo_ref[wa + S5_WIDTH:, :]))
    x1 = x_ref[...] + gt_ref[0] * o
    x1_ref[...] = x1
    h2 = _rms(x1, gffn_ref[...]) * (1.0 + sc_ref[0]) + sh_ref[0]
    h2_ref[...] = h2
    hi, lo = _split_bf16(h2)
    logits = _dot(hi, wrh_ref[...]) + _dot(hi, wrl_ref[...]) + _dot(lo, wrh_ref[...]) + br_ref[...]
    lane = lax.broadcasted_iota(jnp.int32, logits.shape, 1)
    lanef = lane.astype(F32)
    vals, idxs = [], []
    for _ in range(TOP_K):
        m = jnp.max(logits, axis=-1, keepdims=True)
        idx = jnp.min(jnp.where(logits == m, lanef, float(ROUTER_PAD)), axis=-1, keepdims=True).astype(jnp.int32)
        vals.append(m)
        idxs.append(idx)
        logits = jnp.where(lane == idx, -jnp.inf, logits)
    es = [jnp.exp(v - vals[0]) for v in vals]
    den = es[0] + es[1] + es[2] + es[3]
    ti = jnp.zeros(lane.shape, jnp.int32)
    tg = jnp.zeros(lane.shape, F32)
    for kk in range(TOP_K):
        ti = jnp.where(lane == kk, idxs[kk], ti)
        tg = jnp.where(lane == kk, es[kk] / den, tg)
    ti_ref[...] = ti
    tg_ref[...] = tg


def _outproj(x, oa, ob, oc, gt, sc, sh, lw, *, B, T, tm):
    N = B * T
    nT = T // tm
    mt = gt.shape[1]
    row = lambda b, t: (b * nT + t, 0)
    full = lambda b, t: (0, 0)
    if mt == 1:
        mod_spec = pl.BlockSpec((1, 1, D_MODEL), lambda b, t: (b, 0, 0))
    else:
        mod_spec = pl.BlockSpec((1, tm, D_MODEL), lambda b, t: (b, t, 0))
    weights = [lw['w_out'], lw['g_ffn'], lw['wr_hi'], lw['wr_lo'], lw['b_router']]
    return pl.pallas_call(
        _out_body, grid=(B, nT),
        in_specs=[pl.BlockSpec((tm, D_MODEL), row), pl.BlockSpec((tm, MLA_HEADS * MLA_V), row),
                  pl.BlockSpec((tm, S5_WIDTH), lambda b, t: (t, b)), pl.BlockSpec((tm, SB_WIDTH), row),
                  mod_spec, mod_spec, mod_spec] + [pl.BlockSpec(w.shape, full) for w in weights],
        out_specs=(pl.BlockSpec((tm, D_MODEL), row), pl.BlockSpec((tm, D_MODEL), row),
                   pl.BlockSpec((tm, ROUTER_PAD), row), pl.BlockSpec((tm, ROUTER_PAD), row)),
        out_shape=(jax.ShapeDtypeStruct((N, D_MODEL), F32), jax.ShapeDtypeStruct((N, D_MODEL), F32),
                   jax.ShapeDtypeStruct((N, ROUTER_PAD), jnp.int32), jax.ShapeDtypeStruct((N, ROUTER_PAD), F32)),
        compiler_params=_cparams(("parallel", "parallel")),
    )(x, oa, ob, oc, gt, sc, sh, *weights)


def _moe_body(be_ref, nu_ref, x_ref, wg_ref, wu_ref, wd_ref, bg_ref, bu_ref, bd_ref, o_ref):
    i = pl.program_id(0)

    @pl.when(i < nu_ref[0])
    def _():
        x = x_ref[...].astype(BF16)
        g = jnp.minimum(_dot(x, wg_ref[0]) + bg_ref[0], SWIGLU_LIMIT)
        up = jnp.clip(_dot(x, wu_ref[0]) + bu_ref[0], -SWIGLU_LIMIT, SWIGLU_LIMIT)
        act = (up + 1.0) * (g * jax.nn.sigmoid(SWIGLU_ALPHA * g))
        o_ref[...] = _dot(act.astype(BF16), wd_ref[0]) + bd_ref[0]

    @pl.when(i >= nu_ref[0])
    def _():
        o_ref[...] = jnp.zeros(o_ref.shape, F32)


def _moe(block_expert, n_used, xs, lw, *, bm):
    n_slots = xs.shape[0]
    n_blocks = n_slots // bm
    wsel = lambda i, be, nu: (be[i], 0, 0)
    row = lambda i, be, nu: (i, 0)
    return pl.pallas_call(
        _moe_body,
        grid_spec=pltpu.PrefetchScalarGridSpec(
            num_scalar_prefetch=2, grid=(n_blocks,),
            in_specs=[pl.BlockSpec((bm, D_MODEL), row),
                      pl.BlockSpec((1, D_MODEL, D_FF), wsel), pl.BlockSpec((1, D_MODEL, D_FF), wsel),
                      pl.BlockSpec((1, D_FF, D_MODEL), wsel),
                      pl.BlockSpec((1, 1, D_FF), wsel), pl.BlockSpec((1, 1, D_FF), wsel),
                      pl.BlockSpec((1, 1, D_MODEL), wsel)],
            out_specs=pl.BlockSpec((bm, D_MODEL), row)),
        out_shape=jax.ShapeDtypeStruct((n_slots, D_MODEL), F32),
        compiler_params=_cparams(("arbitrary",)),
    )(block_expert, n_used, xs, lw['w_gate'], lw['w_up'], lw['w_down'], lw['b_gate'], lw['b_up'], lw['b_down'])


def _route(topi, bm):
    n = topi.shape[0]
    tk = n * TOP_K
    flat_e = topi[:, :TOP_K].reshape(tk)
    order = jnp.argsort(flat_e).astype(jnp.int32)
    inv = jnp.argsort(order).astype(jnp.int32)
    onehot = flat_e[:, None] == jnp.arange(N_EXPERTS, dtype=jnp.int32)[None, :]
    counts = jnp.sum(onehot.astype(jnp.int32), axis=0)
    blocks_per_e = (counts + bm - 1) // bm
    block_end = jnp.cumsum(blocks_per_e)
    block_start = block_end - blocks_per_e
    group_start = jnp.cumsum(counts) - counts
    shift = block_start * bm - group_start
    dest = inv + jnp.sum(jnp.where(onehot, shift[None, :], 0), axis=1)
    n_blocks = -(-tk // bm) + N_EXPERTS
    blk = jnp.arange(n_blocks, dtype=jnp.int32)
    block_expert = jnp.minimum(jnp.sum((block_end[None, :] <= blk[:, None]).astype(jnp.int32), axis=1),
                               N_EXPERTS - 1)
    within = (blk - block_start[block_expert]) * bm
    first = jnp.clip(group_start[block_expert] + within, 0, tk)
    left = counts[block_expert] - within
    order_pad = jnp.concatenate([order, jnp.zeros((bm,), jnp.int32)])
    rows = jax.vmap(lambda st: lax.dynamic_slice(order_pad, (st,), (bm,)))(first)
    valid = jnp.arange(bm, dtype=jnp.int32)[None, :] < left[:, None]
    slot_tok = jnp.where(valid, rows // TOP_K, 0).reshape(n_blocks * bm)
    return slot_tok, dest, block_expert, block_end[-1:].astype(jnp.int32)


SC_WINDOW = 32


def _gather_rows(x, idx):
    n_idx = idx.shape[0]
    d = x.shape[1]
    info = plsc.get_sparse_core_info()
    nc, ns = info.num_cores, info.num_subcores
    per_w = n_idx // (nc * ns)
    if n_idx % (nc * ns * SC_WINDOW) != 0:
        return jnp.take(x, idx, axis=0)
    mesh = plsc.VectorSubcoreMesh(core_axis_name="c", subcore_axis_name="s")

    @functools.partial(
        pl.kernel, out_type=jax.ShapeDtypeStruct((n_idx, d), x.dtype), mesh=mesh,
        scratch_types=[pltpu.VMEM((SC_WINDOW,), jnp.int32), pltpu.VMEM((SC_WINDOW, d), x.dtype),
                       pltpu.SemaphoreType.DMA])
    def gather_kernel(x_hbm, i_hbm, o_hbm, idx_v, rows_v, sem):
        base = (lax.axis_index("s") * nc + lax.axis_index("c")) * per_w

        @pl.loop(0, per_w // SC_WINDOW)
        def _(j):
            off = base + j * SC_WINDOW
            pltpu.sync_copy(i_hbm.at[pl.ds(off, SC_WINDOW)], idx_v)
            pltpu.async_copy(x_hbm.at[idx_v], rows_v, sem).wait()
            pltpu.sync_copy(rows_v, o_hbm.at[pl.ds(off, SC_WINDOW)])

    return gather_kernel(x, idx)


def _combine_body(x_ref, y_ref, g_ref, gt_ref, o_ref):
    g = g_ref[...]
    y = y_ref[0] * g[:, 0:1]
    for kk in range(1, TOP_K):
        y = y + y_ref[kk] * g[:, kk:kk + 1]
    o_ref[...] = x_ref[...] + gt_ref[0] * y


def _combine(x1, yk, topg, gt, *, B, T, tm):
    nT = T // tm
    mt = gt.shape[1]
    row = lambda b, t: (b * nT + t, 0)
    if mt == 1:
        mod_spec = pl.BlockSpec((1, 1, D_MODEL), lambda b, t: (b, 0, 0))
    else:
        mod_spec = pl.BlockSpec((1, tm, D_MODEL), lambda b, t: (b, t, 0))
    return pl.pallas_call(
        _combine_body, grid=(B, nT),
        in_specs=[pl.BlockSpec((tm, D_MODEL), row),
                  pl.BlockSpec((TOP_K, tm, D_MODEL), lambda b, t: (0, b * nT + t, 0)),
                  pl.BlockSpec((tm, ROUTER_PAD), row), mod_spec],
        out_specs=pl.BlockSpec((tm, D_MODEL), row),
        out_shape=jax.ShapeDtypeStruct(x1.shape, F32),
        compiler_params=_cparams(("parallel", "parallel")),
    )(x1, yk, topg, gt)


def _page_copies(pt_ref, b, c, slot, srcs, bufs, sems, keys_on_lanes, *, layer, G):
    out = []
    for g in range(G):
        page = pt_ref[b, c * G + g]
        for src, buf, sem, on_lanes in zip(srcs, bufs, sems, keys_on_lanes):
            keys = pl.ds(g * PAGE_SIZE, PAGE_SIZE)
            dst = buf.at[slot, :, keys] if on_lanes else buf.at[slot, keys]
            out.append(pltpu.make_async_copy(src.at[layer, page], dst, sem.at[slot]))
    return out


def _mla_dec_body(pt_ref, qg_ref, qpe_ref, q8_ref, k8_ref, cn_ref, wukt_ref, wuv_ref, kv_hbm, kr_hbm,
                  o_ref, kvbuf, krbuf, sem_kv, sem_kr, m_ref, l_ref, pc_ref, *, layer, G, nc, nb, scale):
    b = pl.program_id(0)
    c = pl.program_id(1)
    step = b * nc + c
    slot = lax.rem(step, 2)
    copies = functools.partial(_page_copies, pt_ref, srcs=(kv_hbm, kr_hbm), bufs=(kvbuf, krbuf),
                               sems=(sem_kv, sem_kr), keys_on_lanes=(False, True), layer=layer, G=G)

    @pl.when(step == 0)
    def _():
        for cp in copies(0, 0, 0):
            cp.start()

    @pl.when(step + 1 < nb * nc)
    def _():
        nxt = step + 1
        for cp in copies(nxt // nc, lax.rem(nxt, nc), 1 - slot):
            cp.start()

    @pl.when(c == 0)
    def _():
        m_ref[...] = jnp.full(m_ref.shape, -jnp.inf, F32)
        l_ref[...] = jnp.zeros(l_ref.shape, F32)
        pc_ref[...] = jnp.zeros(pc_ref.shape, F32)

    for cp in copies(b, c, slot):
        cp.wait()

    cb = kvbuf[slot].astype(BF16)
    knt = _dot_nt(wukt_ref[...], cb)
    ss, sn = [], []
    for h in range(MLA_HEADS):
        kh = knt[h * MLA_NOPE:(h + 1) * MLA_NOPE, :]
        ss.append(jnp.sum(kh * kh, axis=0, keepdims=True))
        sn.append(jnp.sum(kh * qg_ref[0, h * MLA_NOPE:(h + 1) * MLA_NOPE, :], axis=0, keepdims=True))
    pad = jnp.zeros((8 - MLA_HEADS, knt.shape[1]), F32)
    ss = jnp.concatenate(ss + [pad], axis=0)
    sn = jnp.concatenate(sn + [pad], axis=0)
    s = sn * lax.rsqrt(ss * (1.0 / MLA_NOPE) + EPS)
    s = (s + _dot(qpe_ref[0], krbuf[slot].astype(BF16))) * scale
    m_prev = m_ref[...]
    m_new = jnp.maximum(m_prev, jnp.max(s, axis=-1, keepdims=True))
    alpha = jnp.exp(m_prev - m_new)
    p = jnp.exp(s - m_new[:, :1])
    l_ref[...] = alpha * l_ref[...] + jnp.sum(p, axis=-1, keepdims=True)
    pc_ref[...] = alpha * pc_ref[...] + _dot(p.astype(BF16), cb)
    m_ref[...] = m_new

    @pl.when(c == nc - 1)
    def _():
        s_new = jnp.sum(q8_ref[0].astype(F32) * k8_ref[0].astype(F32), axis=-1, keepdims=True) * scale
        m_f = jnp.maximum(m_ref[...], s_new)
        a_old = jnp.exp(m_ref[...] - m_f)
        p_new = jnp.exp(s_new - m_f)
        l_f = a_old * l_ref[...] + p_new
        ctx = (a_old * pc_ref[...] + p_new * cn_ref[0]) / l_f
        full = _dot(ctx.astype(BF16), wuv_ref[...])
        rowi = lax.broadcasted_iota(jnp.int32, full.shape, 0)
        coli = lax.broadcasted_iota(jnp.int32, full.shape, 1)
        own = (coli >= rowi * MLA_V) & (coli < (rowi + 1) * MLA_V)
        o_ref[0] = jnp.sum(jnp.where(own, full, 0.0), axis=0, keepdims=True).astype(o_ref.dtype)


def _mla_decode(page_table, qg, qpe, q8, k8, cn, lw, cache_kv, cache_krt, *, layer):
    nb, n_pages = page_table.shape
    G = min(16, n_pages)
    nc = n_pages // G
    rows = G * PAGE_SIZE
    scale = (MLA_NOPE + MLA_ROPE) ** -0.5
    per_b = lambda b, c, pt: (b, 0, 0)
    full = lambda b, c, pt: (0, 0)
    return pl.pallas_call(
        functools.partial(_mla_dec_body, layer=layer, G=G, nc=nc, nb=nb, scale=scale),
        grid_spec=pltpu.PrefetchScalarGridSpec(
            num_scalar_prefetch=1, grid=(nb, nc),
            in_specs=[pl.BlockSpec((1, MLA_HEADS * MLA_NOPE, 1), per_b), pl.BlockSpec((1, 8, MLA_ROPE), per_b),
                      pl.BlockSpec((1, 8, HEAD_PAD), per_b), pl.BlockSpec((1, 8, HEAD_PAD), per_b),
                      pl.BlockSpec((1, 1, MLA_KV_LORA), per_b),
                      pl.BlockSpec(lw['wukt'].shape, full), pl.BlockSpec(lw['wuv'].shape, full),
                      pl.BlockSpec(memory_space=pl.ANY), pl.BlockSpec(memory_space=pl.ANY)],
            out_specs=pl.BlockSpec((1, 1, MLA_HEADS * MLA_V), per_b),
            scratch_shapes=[pltpu.VMEM((2, rows, MLA_KV_LORA), F32), pltpu.VMEM((2, MLA_ROPE, rows), F32),
                            pltpu.SemaphoreType.DMA((2,)), pltpu.SemaphoreType.DMA((2,)),
                            pltpu.VMEM((8, LANE), F32), pltpu.VMEM((8, LANE), F32),
                            pltpu.VMEM((8, MLA_KV_LORA), F32)]),
        out_shape=jax.ShapeDtypeStruct((nb, 1, MLA_HEADS * MLA_V), BF16),
        compiler_params=_cparams(("arbitrary", "arbitrary")),
    )(page_table, qg, qpe, q8, k8, cn, lw['wukt'], lw['wuv'], cache_kv, cache_krt)


def _sb_dec_body(pt_ref, q_ref, uo_ref, sfx_ref, k_hbm, v_hbm, o_ref, kbuf, vbuf, sem_k, sem_v, acc_ref, c_ref,
                 *, layer, G, nc, nb):
    b = pl.program_id(0)
    ci = pl.program_id(1)
    step = b * nc + ci
    slot = lax.rem(step, 2)
    copies = functools.partial(_page_copies, pt_ref, srcs=(k_hbm, v_hbm), bufs=(kbuf, vbuf),
                               sems=(sem_k, sem_v), keys_on_lanes=(True, True), layer=layer, G=G)

    @pl.when(step == 0)
    def _():
        for cp in copies(0, nc - 1, 0):
            cp.start()

    @pl.when(step + 1 < nb * nc)
    def _():
        nxt = step + 1
        for cp in copies(nxt // nc, nc - 1 - lax.rem(nxt, nc), 1 - slot):
            cp.start()

    @pl.when(ci == 0)
    def _():
        acc_ref[...] = jnp.zeros(acc_ref.shape, F32)
        c_ref[...] = jnp.zeros(c_ref.shape, F32)

    for cp in copies(b, nc - 1 - ci, slot):
        cp.wait()

    q = q_ref[0]
    kt = kbuf[slot].astype(BF16)
    vt = vbuf[slot].astype(BF16)
    page = lambda a, g: a[:, g * PAGE_SIZE:(g + 1) * PAGE_SIZE]
    z = jnp.concatenate([_dot(q, page(kt, g)) for g in range(G)], axis=0)
    lb, lk = _log_sigmoid_pair(z)
    hi, lo = _split_bf16(lk)
    r = _dot(hi, uo_ref[...]) + _dot(lo, uo_ref[...])
    th, tl = _split_bf16(r[:, PAGE_SIZE:])
    later = _dot(sfx_ref[...], th) + _dot(sfx_ref[...], tl)
    c = c_ref[...]
    w = jnp.exp(lb + r[:, :PAGE_SIZE] + later + jnp.concatenate([c] * G, axis=0))
    acc = acc_ref[...]
    for g in range(G):
        acc = acc + _dot_nt(w[g * 8:(g + 1) * 8, :].astype(BF16), page(vt, g))
    acc_ref[...] = acc
    c_ref[...] = c + later[0:8, :] + r[0:8, PAGE_SIZE:]

    @pl.when(ci == nc - 1)
    def _():
        acc = acc_ref[...]
        rowi = lax.broadcasted_iota(jnp.int32, acc.shape, 0)
        coli = lax.broadcasted_iota(jnp.int32, acc.shape, 1)
        own = (coli >= rowi * SB_HEAD_DIM) & (coli < (rowi + 1) * SB_HEAD_DIM)
        o_ref[0] = jnp.sum(jnp.where(own, acc, 0.0), axis=0, keepdims=True).astype(o_ref.dtype)


def _sb_decode(page_table, qs, cache_kt, cache_vt, *, layer):
    nb, n_pages = page_table.shape
    G = min(16, n_pages)
    nc = n_pages // G
    rows = G * PAGE_SIZE
    r = np.arange(8 * G)
    sfx = jnp.asarray((r[None, :] % 8 == r[:, None] % 8) & (r[None, :] // 8 > r[:, None] // 8), BF16)
    per_b = lambda b, c, pt: (b, 0, 0)
    full = lambda b, c, pt: (0, 0)
    return pl.pallas_call(
        functools.partial(_sb_dec_body, layer=layer, G=G, nc=nc, nb=nb),
        grid_spec=pltpu.PrefetchScalarGridSpec(
            num_scalar_prefetch=1, grid=(nb, nc),
            in_specs=[pl.BlockSpec((1, 8, SB_WIDTH), per_b), pl.BlockSpec((PAGE_SIZE, 2 * PAGE_SIZE), full),
                      pl.BlockSpec((8 * G, 8 * G), full),
                      pl.BlockSpec(memory_space=pl.ANY), pl.BlockSpec(memory_space=pl.ANY)],
            out_specs=pl.BlockSpec((1, 1, SB_WIDTH), per_b),
            scratch_shapes=[pltpu.VMEM((2, SB_WIDTH, rows), F32), pltpu.VMEM((2, SB_WIDTH, rows), F32),
                            pltpu.SemaphoreType.DMA((2,)), pltpu.SemaphoreType.DMA((2,)),
                            pltpu.VMEM((8, SB_WIDTH), F32), pltpu.VMEM((8, LANE), F32)]),
        out_shape=jax.ShapeDtypeStruct((nb, 1, SB_WIDTH), BF16),
        compiler_params=_cparams(("arbitrary", "arbitrary")),
    )(page_table, qs, _suffix_ones(PAGE_SIZE), sfx, cache_kt, cache_vt)


def _rope_tables(pos):
    half = MLA_ROPE // 2
    inv = ROPE_THETA ** (-jnp.arange(half, dtype=F32) / half)
    ang = pos.astype(F32)[:, None] * inv
    cos, sin = jnp.cos(ang), jnp.sin(ang)
    n = pos.shape[0]
    z16 = jnp.zeros((n, half), F32)
    z32 = jnp.zeros((n, HEAD_PAD - MLA_NOPE - MLA_ROPE), F32)
    z64 = jnp.zeros((n, MLA_NOPE), F32)
    cos_t = jnp.concatenate([jnp.ones((n, MLA_NOPE), F32), cos, cos, z32], axis=1)
    sin_a = jnp.concatenate([z64, -sin, z16, z32], axis=1)
    sin_b = jnp.concatenate([z64, z16, sin, z32], axis=1)
    return cos_t, sin_a, sin_b


def _pad_heads(w, width):
    K = w.shape[0]
    w = w.reshape(K, MLA_HEADS, width)
    w = jnp.pad(w, ((0, 0), (0, 0), (0, HEAD_PAD - width)))
    return w.reshape(K, MLA_HEADS * HEAD_PAD)


def _block_diag(w):
    a, b = w.shape[1], w.shape[2]
    w = w.reshape(S5_NBLK, S5_BLK_GROUPS, a, b)
    eye = jnp.eye(S5_BLK_GROUPS, dtype=w.dtype)
    out = jnp.einsum('jgab,gh->jgahb', w, eye)
    return out.reshape(S5_NBLK, S5_BLK_GROUPS * a, S5_BLK_GROUPS * b)


def _layer_weights(l, p, disc):
    a_re, a_im, f_re, f_im = [d[l] for d in disc]
    lw = {}
    w_in = p['w_in'][l]
    kpe_cols = jnp.pad(w_in[:, 384:416], ((0, 0), (MLA_NOPE, HEAD_PAD - MLA_NOPE - MLA_ROPE)))
    lw['w1'] = jnp.concatenate([w_in[:, 0:384], kpe_cols, w_in[:, 416:]], axis=1).astype(BF16)
    row = lambda v: v.reshape(1, -1).astype(F32)
    lw['g_mix'] = row(p['g_mix'][l])
    lw['g_q_lat'] = row(p['g_q_lat'][l])
    lw['g_kv_lat'] = row(p['g_kv_lat'][l])
    lw['wuq'] = _pad_heads(p['w_uq'][l], MLA_NOPE + MLA_ROPE).astype(BF16)
    lw['wuk'] = _pad_heads(p['w_uk'][l], MLA_NOPE).astype(BF16)
    lw['wukt'] = p['w_uk'][l].T.astype(BF16)
    lw['wuv'] = p['w_uv'][l].astype(BF16)
    zpad = jnp.zeros((HEAD_PAD - MLA_NOPE - MLA_ROPE,), F32)
    lw['gq'] = row(jnp.concatenate([p['g_qn_nope'][l], p['g_qn_rope'][l], zpad]))
    lw['gk'] = row(jnp.concatenate([p['g_kn_nope'][l], jnp.zeros((HEAD_PAD - MLA_NOPE,), F32)]))
    lw['gkr'] = row(jnp.concatenate([jnp.zeros((MLA_NOPE,), F32), p['g_kn_rope'][l], zpad]))
    br, bi = p['s5_b_re'][l], p['s5_b_im'][l]
    bb_re = f_re[..., None] * br - f_im[..., None] * bi
    bb_im = f_re[..., None] * bi + f_im[..., None] * br
    lw['bb_re'] = _block_diag(jnp.swapaxes(bb_re, 1, 2)).astype(BF16)
    lw['bb_im'] = _block_diag(jnp.swapaxes(bb_im, 1, 2)).astype(BF16)
    lw['cc_re'] = _block_diag(jnp.swapaxes(p['s5_c_re'][l], 1, 2)).astype(BF16)
    lw['cc_im'] = _block_diag(jnp.swapaxes(p['s5_c_im'][l], 1, 2)).astype(BF16)
    lw['a_re'] = a_re.reshape(1, S5_LANES)
    lw['a_im'] = a_im.reshape(1, S5_LANES)
    lw['s5_d'] = row(p['s5_d'][l])
    lw['w_glu'] = p['w_glu'][l].astype(BF16)
    lw['b_glu'] = row(p['b_glu'][l])
    lw['w_out'] = p['w_out'][l].astype(BF16)
    lw['g_ffn'] = row(p['g_ffn'][l])
    wr = jnp.pad(p['w_router'][l], ((0, 0), (0, ROUTER_PAD - N_EXPERTS)))
    lw['wr_hi'] = wr.astype(BF16)
    lw['wr_lo'] = (wr - lw['wr_hi'].astype(F32)).astype(BF16)
    lw['b_router'] = row(jnp.concatenate([p['b_router'][l], jnp.full((ROUTER_PAD - N_EXPERTS,), NEG_BIG, F32)]))
    w_gu = p['w_gu'][l].reshape(N_EXPERTS, D_MODEL, D_FF, 2)
    lw['w_gate'] = w_gu[..., 0].astype(BF16)
    lw['w_up'] = w_gu[..., 1].astype(BF16)
    lw['w_down'] = p['w_down'][l].astype(BF16)
    b_gu = p['b_gu'][l].reshape(N_EXPERTS, 1, D_FF, 2)
    lw['b_gate'] = b_gu[..., 0]
    lw['b_up'] = b_gu[..., 1]
    lw['b_down'] = p['b_down'][l].reshape(N_EXPERTS, 1, D_MODEL)
    return lw


def _split_mod(mod):
    return [mod[..., i * D_MODEL:(i + 1) * D_MODEL] for i in range(N_MOD)]


def _ffn(x1, h2, topi, topg, gt_f, lw, *, B, T, tm, bm):
    n = B * T
    slot_tok, dest, block_expert, n_used = _route(topi, bm)
    xs = _gather_rows(h2, slot_tok)
    ys = _moe(block_expert, n_used, xs, lw, bm=bm)
    yk = _gather_rows(ys, dest.reshape(n, TOP_K).T.reshape(n * TOP_K)).reshape(TOP_K, n, D_MODEL)
    return _combine(x1, yk, topg, gt_f, B=B, T=T, tm=tm)


def _prompt_layer(x, mods, tabs, lw, *, B, T):
    sh_a, sc_a, gt_a, sh_f, sc_f, gt_f = mods
    tm = min(256, T)
    q, k, v, ckv, kpe, u, sq, sk16, sv16, sk, sv = _proj(x, sc_a, sh_a, tabs, lw, B=B, T=T, tm=tm, qk_dtype=BF16)
    o_a = _mla_prompt(q, k, v, B=B, T=T, tq=512 if T % 512 == 0 else min(256, T))
    o_c = _sb_prompt(sq, sk16, sv16, B=B, T=T, tq=min(256, T))
    zeros = jnp.zeros((B, S5_LANES), F32)
    Tc = min(64, T)
    o_b, s_re, s_im = _s5(u.reshape(T * B, S5_WIDTH), zeros, zeros, lw, B=B, T=T, Tc=Tc)
    x1, h2, topi, topg = _outproj(x, o_a, o_b.reshape(T, B * S5_WIDTH), o_c, gt_a, sc_f, sh_f, lw, B=B, T=T, tm=tm)
    x2 = _ffn(x1, h2, topi, topg, gt_f, lw, B=B, T=T, tm=tm, bm=min(256, max(16, (B * T * TOP_K) // N_EXPERTS)))
    state = (ckv.reshape(B, T, MLA_KV_LORA), kpe.reshape(B, T, MLA_ROPE),
             sk.reshape(B, T, SB_HEADS, SB_HEAD_DIM), sv.reshape(B, T, SB_HEADS, SB_HEAD_DIM),
             s_re.reshape(B, S5_GROUPS, S5_STATE), s_im.reshape(B, S5_GROUPS, S5_STATE))
    return x2, state


def _sample_layer(x, mods, tabs, lw, caches, h0, page_table, p, *, l, DB):
    sh_a, sc_a, gt_a, sh_f, sc_f, gt_f = mods
    cache_kv, cache_kr, cache_sk, cache_sv = caches
    q, k, v, ckv, kpe, u, sq, sk16, sv16, sk, sv = _proj(x, sc_a, sh_a, tabs, lw, B=1, T=DB, tm=DB, qk_dtype=F32)
    q8 = jnp.pad(q.reshape(DB, MLA_HEADS, HEAD_PAD), ((0, 0), (0, 8 - MLA_HEADS), (0, 0)))
    k8 = jnp.pad(k.reshape(DB, MLA_HEADS, HEAD_PAD), ((0, 0), (0, 8 - MLA_HEADS), (0, 0)))
    qg = (q8[:, :MLA_HEADS, :MLA_NOPE] * p['g_kn_nope'][l][None, None, :]).reshape(DB, MLA_HEADS * MLA_NOPE, 1)
    qpe = q8[:, :, MLA_NOPE:MLA_NOPE + MLA_ROPE].astype(BF16)
    o_a = _mla_decode(page_table, qg, qpe, q8.astype(BF16), k8.astype(BF16), ckv.reshape(DB, 1, MLA_KV_LORA),
                      lw, cache_kv, cache_kr, layer=l).reshape(DB, MLA_HEADS * MLA_V)
    sq4 = sq.reshape(DB, SB_HEADS, SB_HEAD_DIM)
    eye4 = jnp.eye(8, SB_HEADS, dtype=BF16)
    qs = jnp.einsum('bhd,rh,hg->brgd', sq4, eye4, jnp.eye(SB_HEADS, dtype=BF16)).reshape(DB, 8, SB_WIDTH)
    o_c = _sb_decode(page_table, qs, cache_sk, cache_sv, layer=l).reshape(DB, SB_WIDTH)
    o_b, s_re, s_im = _s5(u, h0[0], h0[1], lw, B=DB, T=1, Tc=1)
    x1, h2, topi, topg = _outproj(x, o_a, o_b, o_c, gt_a, sc_f, sh_f, lw, B=1, T=DB, tm=DB)
    x2 = _ffn(x1, h2, topi, topg, gt_f, lw, B=1, T=DB, tm=DB, bm=16)
    state = (ckv.reshape(DB, 1, MLA_KV_LORA), kpe.reshape(DB, 1, MLA_ROPE),
             sk.reshape(DB, 1, SB_HEADS, SB_HEAD_DIM), sv.reshape(DB, 1, SB_HEADS, SB_HEAD_DIM),
             s_re.reshape(DB, S5_GROUPS, S5_STATE), s_im.reshape(DB, S5_GROUPS, S5_STATE))
    return x2, state


def kernel(x_prompt, x_sample, cache_kv_latent, cache_k_rope, cache_sb_k, cache_sb_v, state_s5_re, state_s5_im, page_table, c_prompt, c_sample, w_ada, b_ada, g_mix, w_in, g_q_lat, w_uq, g_qn_nope, g_qn_rope, g_kv_lat, g_kn_rope, w_uk, w_uv, g_kn_nope, s5_lam_re, s5_lam_im, s5_log_step, s5_b_re, s5_b_im, s5_c_re, s5_c_im, s5_d, w_glu, b_glu, w_out, g_ffn, w_router, b_router, w_gu, b_gu, w_down, b_down):
    B, T, _ = x_prompt.shape
    DB, DT, _ = x_sample.shape
    assert DT == 1, "the sample group carries one new token per row"
    L = w_ada.shape[0]
    n_pool = cache_kv_latent.shape[1]
    past_len = page_table.shape[1] * PAGE_SIZE
    p = dict(w_in=w_in, g_mix=g_mix, g_q_lat=g_q_lat, w_uq=w_uq, g_qn_nope=g_qn_nope, g_qn_rope=g_qn_rope,
             g_kv_lat=g_kv_lat, g_kn_rope=g_kn_rope, w_uk=w_uk, w_uv=w_uv, g_kn_nope=g_kn_nope,
             s5_b_re=s5_b_re, s5_b_im=s5_b_im, s5_c_re=s5_c_re, s5_c_im=s5_c_im, s5_d=s5_d, w_glu=w_glu,
             b_glu=b_glu, w_out=w_out, g_ffn=g_ffn, w_router=w_router, b_router=b_router, w_gu=w_gu, b_gu=b_gu,
             w_down=w_down, b_down=b_down)
    mod = _ada(jnp.concatenate([c_prompt, c_sample], axis=0), w_ada, b_ada)
    disc = _s5disc(s5_lam_re, s5_lam_im, s5_log_step)
    tabs_p = _rope_tables(jnp.arange(T, dtype=jnp.int32))
    tabs_s = _rope_tables(jnp.full((DB,), past_len, jnp.int32))
    caches = (cache_kv_latent, jnp.transpose(cache_k_rope, (0, 1, 3, 2)),
              jnp.transpose(cache_sb_k, (0, 1, 3, 4, 2)).reshape(L, n_pool, SB_WIDTH, PAGE_SIZE),
              jnp.transpose(cache_sb_v, (0, 1, 3, 4, 2)).reshape(L, n_pool, SB_WIDTH, PAGE_SIZE))
    xp = x_prompt.reshape(B * T, D_MODEL)
    xs = x_sample.reshape(DB, D_MODEL)
    p_rows, s_rows = [], []
    for l in range(L):
        lw = _layer_weights(l, p, disc)
        mods_p = _split_mod(mod[l, :B].reshape(B, 1, N_MOD * D_MODEL))
        mods_s = _split_mod(mod[l, B:].reshape(1, DB, N_MOD * D_MODEL))
        xp, st_p = _prompt_layer(xp, mods_p, tabs_p, lw, B=B, T=T)
        p_rows.append(st_p)
        h0 = (state_s5_re[l].reshape(DB, S5_LANES), state_s5_im[l].reshape(DB, S5_LANES))
        xs, st_s = _sample_layer(xs, mods_s, tabs_s, lw, caches, h0, page_table, p, l=l, DB=DB)
        s_rows.append(st_s)
    p_out = [jnp.stack(a) for a in zip(*p_rows)]
    s_out = [jnp.stack(a) for a in zip(*s_rows)]
    return (xp.reshape(B, T, D_MODEL), xs.reshape(DB, 1, D_MODEL), *p_out, *s_out)
```

```python
import functools

import jax
import jax.numpy as jnp
import numpy as np
from jax import lax
from jax.experimental import pallas as pl
from jax.experimental.pallas import tpu as pltpu
from jax.experimental.pallas import tpu_sc as plsc

F32 = jnp.float32
BF16 = jnp.bfloat16

D_MODEL = 1024
PAGE_SIZE = 128
MLA_HEADS = 6
MLA_NOPE = 64
MLA_ROPE = 32
MLA_V = 64
MLA_Q_LORA = 256
MLA_KV_LORA = 128
ROPE_THETA = 10000.0
S5_CH = 16
S5_GROUPS = 24
S5_STATE = 64
S5_WIDTH = S5_GROUPS * S5_CH
S5_LANES = S5_GROUPS * S5_STATE
SB_HEADS = 4
SB_HEAD_DIM = 64
SB_WIDTH = SB_HEADS * SB_HEAD_DIM
N_EXPERTS = 32
TOP_K = 4
D_FF = 1024
SWIGLU_LIMIT = 7.0
SWIGLU_ALPHA = 1.702
N_MOD = 6
EPS = 1e-6

LANE = 128
HEAD_PAD = 128
S5_BLK_GROUPS = LANE // S5_CH
S5_NBLK = S5_GROUPS // S5_BLK_GROUPS
S5_BLK_LANES = S5_BLK_GROUPS * S5_STATE
ROUTER_PAD = 128
NEG_BIG = -1e30
VMEM_LIMIT = 56 * 1024 * 1024

_C_Q, _C_KV, _C_KPE, _C_U, _C_SQ, _C_SK, _C_SV, _C_END = 0, 256, 384, 512, 896, 1152, 1408, 1664


def _cparams(sem):
    return pltpu.CompilerParams(dimension_semantics=sem, vmem_limit_bytes=VMEM_LIMIT)


def _dot(a, b):
    return jnp.dot(a, b, preferred_element_type=F32)


def _dot_nt(a, b):
    return lax.dot_general(a, b, (((1,), (1,)), ((), ())), preferred_element_type=F32)


def _rms(x, g):
    return x * lax.rsqrt(jnp.mean(x * x, axis=-1, keepdims=True) + EPS) * g


def _rope_lanes(x, cos, sin_a, sin_b):
    return x * cos + pltpu.roll(x, HEAD_PAD - MLA_ROPE // 2, 1) * sin_a + pltpu.roll(x, MLA_ROPE // 2, 1) * sin_b


def _ada_body(c_ref, w_ref, b_ref, o_ref):
    c = c_ref[...]
    s = (c * jax.nn.sigmoid(c)).astype(BF16)
    o_ref[0] = _dot(s, w_ref[0].astype(BF16)) + b_ref[0]


def _ada(c_all, w_ada, b_ada):
    L = w_ada.shape[0]
    R = c_all.shape[0]
    ncol = N_MOD * D_MODEL // D_MODEL
    return pl.pallas_call(
        _ada_body,
        grid=(L, ncol),
        in_specs=[pl.BlockSpec((R, D_MODEL), lambda l, j: (0, 0)),
                  pl.BlockSpec((1, D_MODEL, D_MODEL), lambda l, j: (l, 0, j)),
                  pl.BlockSpec((1, 1, D_MODEL), lambda l, j: (l, 0, j))],
        out_specs=pl.BlockSpec((1, R, D_MODEL), lambda l, j: (l, 0, j)),
        out_shape=jax.ShapeDtypeStruct((L, R, N_MOD * D_MODEL), F32),
        compiler_params=_cparams(("parallel", "parallel")),
    )(c_all, w_ada, b_ada.reshape(L, 1, N_MOD * D_MODEL))


def _s5disc_body(lr_ref, li_ref, ls_ref, ar_ref, ai_ref, fr_ref, fi_ref):
    lr, li = lr_ref[...], li_ref[...]
    dt = jnp.exp(ls_ref[...])
    mag = jnp.exp(lr * dt)
    a_re, a_im = mag * jnp.cos(li * dt), mag * jnp.sin(li * dt)
    nr, ni = a_re - 1.0, a_im
    den = lr * lr + li * li
    ar_ref[...] = a_re
    ai_ref[...] = a_im
    fr_ref[...] = (nr * lr + ni * li) / den
    fi_ref[...] = (ni * lr - nr * li) / den


def _s5disc(lam_re, lam_im, log_step):
    L, G, N = lam_re.shape
    R = L * G
    sds = jax.ShapeDtypeStruct((R, N), F32)
    outs = pl.pallas_call(_s5disc_body, out_shape=(sds, sds, sds, sds))(
        lam_re.reshape(R, N), lam_im.reshape(R, N), log_step.reshape(R, 1))
    return [o.reshape(L, G, N) for o in outs]


def _proj_body(x_ref, sc_ref, sh_ref, gmix_ref, w1_ref, gql_ref, wuq_ref, gkv_ref, wuk_ref, wuv_ref,
               gq_ref, gk_ref, gkr_ref, cos_ref, sa_ref, sb_ref,
               q_ref, k_ref, v_ref, ckv_ref, kpe_ref, u_ref, sq_ref, sk16_ref, sv16_ref, sk_ref, sv_ref):
    x = x_ref[...]
    hn = _rms(x, gmix_ref[...]) * (1.0 + sc_ref[0]) + sh_ref[0]
    z = _dot(hn.astype(BF16), w1_ref[...])
    tm = x.shape[0]
    lane = lax.broadcasted_iota(jnp.int32, (tm, HEAD_PAD), 1)
    nope = lane < MLA_NOPE
    cos, sin_a, sin_b = cos_ref[...], sa_ref[...], sb_ref[...]

    ckv = _rms(z[:, _C_KV:_C_KPE], gkv_ref[...])
    ckv_ref[...] = ckv
    kraw = z[:, _C_KPE:_C_U]
    kr = kraw * lax.rsqrt(jnp.sum(kraw * kraw, axis=-1, keepdims=True) * (1.0 / MLA_ROPE) + EPS) * gkr_ref[...]
    kr = _rope_lanes(kr, cos, sin_a, sin_b)
    kpe_ref[...] = pltpu.roll(kr, HEAD_PAD - MLA_NOPE, 1)[:, :MLA_ROPE]

    qn = _rms(z[:, _C_Q:_C_KV], gql_ref[...]).astype(BF16)
    qa = _dot(qn, wuq_ref[...])
    cb = ckv.astype(BF16)
    ka = _dot(cb, wuk_ref[...])
    for h in range(MLA_HEADS):
        sl = slice(h * HEAD_PAD, (h + 1) * HEAD_PAD)
        qh = qa[:, sl]
        q2 = qh * qh
        ss_n = jnp.sum(jnp.where(nope, q2, 0.0), axis=-1, keepdims=True) * (1.0 / MLA_NOPE)
        ss_r = jnp.sum(jnp.where(nope, 0.0, q2), axis=-1, keepdims=True) * (1.0 / MLA_ROPE)
        qh = qh * jnp.where(nope, lax.rsqrt(ss_n + EPS), lax.rsqrt(ss_r + EPS)) * gq_ref[...]
        q_ref[:, sl] = _rope_lanes(qh, cos, sin_a, sin_b).astype(q_ref.dtype)
        kh = ka[:, sl]
        kh = kh * lax.rsqrt(jnp.sum(kh * kh, axis=-1, keepdims=True) * (1.0 / MLA_NOPE) + EPS) * gk_ref[...]
        k_ref[:, sl] = (kh + kr).astype(k_ref.dtype)
    v_ref[...] = _dot(cb, wuv_ref[...]).astype(v_ref.dtype)

    u_ref[...] = z[:, _C_U:_C_SQ]
    sq_ref[...] = (z[:, _C_SQ:_C_SK] * (SB_HEAD_DIM ** -0.5)).astype(BF16)
    sk = z[:, _C_SK:_C_SV]
    sv = z[:, _C_SV:_C_END]
    sk_ref[...] = sk
    sv_ref[...] = sv
    sk16_ref[...] = sk.astype(BF16)
    sv16_ref[...] = sv.astype(BF16)


def _proj(x, sc, sh, tabs, lw, *, B, T, tm, qk_dtype):
    N = B * T
    nT = T // tm
    mt = sc.shape[1]
    row = lambda b, t: (b * nT + t, 0)
    full = lambda b, t: (0, 0)
    if mt == 1:
        mod_spec = pl.BlockSpec((1, 1, D_MODEL), lambda b, t: (b, 0, 0))
    else:
        mod_spec = pl.BlockSpec((1, tm, D_MODEL), lambda b, t: (b, t, 0))
    tab_spec = pl.BlockSpec((tm, HEAD_PAD), lambda b, t: (t, 0))

    def wspec(a):
        return pl.BlockSpec(a.shape, full)

    weights = [lw['g_mix'], lw['w1'], lw['g_q_lat'], lw['wuq'], lw['g_kv_lat'], lw['wuk'], lw['wuv'],
               lw['gq'], lw['gk'], lw['gkr']]
    in_specs = ([pl.BlockSpec((tm, D_MODEL), row), mod_spec, mod_spec] + [wspec(w) for w in weights[:1]]
                + [wspec(w) for w in weights[1:]] + [tab_spec] * 3)
    QW = MLA_HEADS * HEAD_PAD
    VW = MLA_HEADS * MLA_V
    out_shape = (
        jax.ShapeDtypeStruct((N, QW), qk_dtype), jax.ShapeDtypeStruct((N, QW), qk_dtype),
        jax.ShapeDtypeStruct((N, VW), BF16),
        jax.ShapeDtypeStruct((N, MLA_KV_LORA), F32), jax.ShapeDtypeStruct((N, MLA_ROPE), F32),
        jax.ShapeDtypeStruct((T, B * S5_WIDTH), F32),
        jax.ShapeDtypeStruct((N, SB_WIDTH), BF16), jax.ShapeDtypeStruct((N, SB_WIDTH), BF16),
        jax.ShapeDtypeStruct((N, SB_WIDTH), BF16),
        jax.ShapeDtypeStruct((N, SB_WIDTH), F32), jax.ShapeDtypeStruct((N, SB_WIDTH), F32))
    out_specs = (
        pl.BlockSpec((tm, QW), row), pl.BlockSpec((tm, QW), row), pl.BlockSpec((tm, VW), row),
        pl.BlockSpec((tm, MLA_KV_LORA), row), pl.BlockSpec((tm, MLA_ROPE), row),
        pl.BlockSpec((tm, S5_WIDTH), lambda b, t: (t, b)),
        pl.BlockSpec((tm, SB_WIDTH), row), pl.BlockSpec((tm, SB_WIDTH), row), pl.BlockSpec((tm, SB_WIDTH), row),
        pl.BlockSpec((tm, SB_WIDTH), row), pl.BlockSpec((tm, SB_WIDTH), row))
    return pl.pallas_call(
        _proj_body, grid=(B, nT), in_specs=in_specs, out_specs=out_specs, out_shape=out_shape,
        compiler_params=_cparams(("parallel", "parallel")),
    )(x, sc, sh, *weights, *tabs)


def _mla_body(q_ref, k_ref, v_ref, o_ref, m_ref, l_ref, acc_ref, *, tq, scale):
    qi = pl.program_id(2)
    m_ref[...] = jnp.full(m_ref.shape, -jnp.inf, F32)
    l_ref[...] = jnp.zeros(l_ref.shape, F32)
    acc_ref[...] = jnp.zeros(acc_ref.shape, F32)
    q = q_ref[...]

    def block(kb, masked):
        r0 = pl.multiple_of(kb * tq, tq)
        k = k_ref[pl.ds(r0, tq), :]
        v = v_ref[pl.ds(r0, tq), :]
        for hh in range(2):
            sl = slice(hh * HEAD_PAD, (hh + 1) * HEAD_PAD)
            s = _dot_nt(q[:, sl], k[:, sl]) * scale
            if masked:
                rowi = lax.broadcasted_iota(jnp.int32, s.shape, 0)
                coli = lax.broadcasted_iota(jnp.int32, s.shape, 1)
                s = jnp.where(coli <= rowi, s, -jnp.inf)
            m_prev = m_ref[hh]
            m_new = jnp.maximum(m_prev, jnp.max(s, axis=-1, keepdims=True))
            alpha = jnp.exp(m_prev - m_new)
            p = jnp.exp(s - m_new[:, :1])
            l_ref[hh] = alpha * l_ref[hh] + jnp.sum(p, axis=-1, keepdims=True)
            acc_ref[hh] = alpha * acc_ref[hh] + _dot(p.astype(BF16), v)
            m_ref[hh] = m_new

    def loop_body(kb, carry):
        block(kb, False)
        return carry

    lax.fori_loop(0, qi, loop_body, 0)
    block(qi, True)
    lane = lax.broadcasted_iota(jnp.int32, (tq, LANE), 1)
    o_ref[...] = jnp.where(lane < MLA_V, acc_ref[0] / l_ref[0], acc_ref[1] / l_ref[1]).astype(o_ref.dtype)


def _mla_prompt(q, k, v, *, B, T, tq):
    nq = T // tq
    npair = MLA_HEADS // 2
    scale = (MLA_NOPE + MLA_ROPE) ** -0.5
    return pl.pallas_call(
        functools.partial(_mla_body, tq=tq, scale=scale),
        grid=(B, npair, nq),
        in_specs=[pl.BlockSpec((tq, 2 * HEAD_PAD), lambda b, h, i: (b * nq + i, h)),
                  pl.BlockSpec((T, 2 * HEAD_PAD), lambda b, h, i: (b, h)),
                  pl.BlockSpec((T, 2 * MLA_V), lambda b, h, i: (b, h))],
        out_specs=pl.BlockSpec((tq, 2 * MLA_V), lambda b, h, i: (b * nq + i, h)),
        out_shape=jax.ShapeDtypeStruct((B * T, MLA_HEADS * MLA_V), BF16),
        scratch_shapes=[pltpu.VMEM((2, tq, LANE), F32), pltpu.VMEM((2, tq, LANE), F32),
                        pltpu.VMEM((2, tq, LANE), F32)],
        compiler_params=_cparams(("parallel", "parallel", "parallel")),
    )(q, k, v)


def _log_sigmoid_pair(z):
    lb = jnp.minimum(z, 0.0) - jnp.log(1.0 + jnp.exp(-jnp.abs(z)))
    return lb, lb - z


def _split_bf16(x):
    hi = x.astype(BF16)
    return hi, (x - hi.astype(F32)).astype(BF16)


def _sb_body(q_ref, k_ref, v_ref, uo_ref, o_ref, acc_ref, c_ref, *, tq):
    qi = pl.program_id(1)
    acc_ref[...] = jnp.zeros(acc_ref.shape, F32)
    c_ref[...] = jnp.zeros(c_ref.shape, F32)
    lane = lax.broadcasted_iota(jnp.int32, (tq, LANE), 1)
    qh = []
    for pr in range(SB_HEADS // 2):
        q = q_ref[:, pr * LANE:(pr + 1) * LANE]
        zero = jnp.zeros_like(q)
        qh += [jnp.where(lane < SB_HEAD_DIM, q, zero), jnp.where(lane < SB_HEAD_DIM, zero, q)]
    uo = uo_ref[...]

    def block(kb, masked):
        r0 = pl.multiple_of(kb * tq, tq)
        for h in range(SB_HEADS):
            pr = h // 2
            k = k_ref[pl.ds(r0, tq), pr * LANE:(pr + 1) * LANE]
            v = v_ref[pl.ds(r0, tq), pr * LANE:(pr + 1) * LANE]
            z = _dot_nt(qh[h], k)
            lb, lk = _log_sigmoid_pair(z)
            if masked:
                rowi = lax.broadcasted_iota(jnp.int32, z.shape, 0)
                coli = lax.broadcasted_iota(jnp.int32, z.shape, 1)
                valid = coli < rowi
                lb = jnp.where(valid, lb, -jnp.inf)
                lk = jnp.where(valid, lk, 0.0)
            hi, lo = _split_bf16(lk)
            r = _dot(hi, uo) + _dot(lo, uo)
            w = jnp.exp(lb + r[:, :tq] + c_ref[h])
            acc_ref[h] = acc_ref[h] + _dot(w.astype(BF16), v)
            c_ref[h] = c_ref[h] + r[:, tq:]

    block(qi, True)

    def loop_body(j, carry):
        block(qi - 1 - j, False)
        return carry

    lax.fori_loop(0, qi, loop_body, 0)
    for pr in range(SB_HEADS // 2):
        o_ref[:, pr * LANE:(pr + 1) * LANE] = jnp.where(
            lane < SB_HEAD_DIM, acc_ref[2 * pr], acc_ref[2 * pr + 1]).astype(o_ref.dtype)


def _suffix_ones(n):
    j = np.arange(n)[:, None]
    s = np.arange(n)[None, :]
    return jnp.asarray(np.concatenate([(j > s), np.ones((n, n), bool)], axis=1), BF16)


def _sb_prompt(q, k, v, *, B, T, tq):
    nq = T // tq
    return pl.pallas_call(
        functools.partial(_sb_body, tq=tq),
        grid=(B, nq),
        in_specs=[pl.BlockSpec((tq, SB_WIDTH), lambda b, i: (b * nq + i, 0)),
                  pl.BlockSpec((T, SB_WIDTH), lambda b, i: (b, 0)),
                  pl.BlockSpec((T, SB_WIDTH), lambda b, i: (b, 0)),
                  pl.BlockSpec((tq, 2 * tq), lambda b, i: (0, 0))],
        out_specs=pl.BlockSpec((tq, SB_WIDTH), lambda b, i: (b * nq + i, 0)),
        out_shape=jax.ShapeDtypeStruct((B * T, SB_WIDTH), BF16),
        scratch_shapes=[pltpu.VMEM((SB_HEADS, tq, LANE), F32), pltpu.VMEM((SB_HEADS, tq, tq), F32)],
        compiler_params=_cparams(("parallel", "parallel")),
    )(q, k, v, _suffix_ones(tq))


def _s5_body(u_ref, h0r_ref, h0i_ref, ar_ref, ai_ref, bbr_ref, bbi_ref, cr_ref, ci_ref, d_ref, wg_ref, bg_ref,
             o_ref, sr_ref, si_ref, hr_s, hi_s, hsr, hsi, *, B, Tc):
    @pl.when(pl.program_id(0) == 0)
    def _():
        hr_s[...] = h0r_ref[...]
        hi_s[...] = h0i_ref[...]

    u = u_ref[...]
    ub = u.astype(BF16)
    for j in range(S5_NBLK):
        uj = ub[:, j * LANE:(j + 1) * LANE]
        hsr[:, j * S5_BLK_LANES:(j + 1) * S5_BLK_LANES] = _dot(uj, bbr_ref[j])
        hsi[:, j * S5_BLK_LANES:(j + 1) * S5_BLK_LANES] = _dot(uj, bbi_ref[j])
    ar, ai = ar_ref[...], ai_ref[...]

    def step(t, carry):
        hr, hi = carry
        r0 = pl.multiple_of(t * B, B)
        nr = ar * hr - ai * hi + hsr[pl.ds(r0, B), :]
        ni = ar * hi + ai * hr + hsi[pl.ds(r0, B), :]
        hsr[pl.ds(r0, B), :] = nr
        hsi[pl.ds(r0, B), :] = ni
        return nr, ni

    hr, hi = lax.fori_loop(0, Tc, step, (hr_s[...], hi_s[...]))
    hr_s[...] = hr
    hi_s[...] = hi
    sr_ref[...] = hr
    si_ref[...] = hi

    ys = []
    for j in range(S5_NBLK):
        sl = slice(j * S5_BLK_LANES, (j + 1) * S5_BLK_LANES)
        ys.append(_dot(hsr[:, sl].astype(BF16), cr_ref[j]) - _dot(hsi[:, sl].astype(BF16), ci_ref[j]))
    y = jnp.concatenate(ys, axis=1) + d_ref[...] * u
    y = jax.nn.gelu(y)
    gate = _dot(y.astype(BF16), wg_ref[...]) + bg_ref[...]
    o_ref[...] = (y * jax.nn.sigmoid(gate)).astype(o_ref.dtype)


def _s5(u, h0r, h0i, lw, *, B, T, Tc):
    rows = Tc * B
    full2 = lambda i: (0, 0)
    full3 = lambda i: (0, 0, 0)
    st = jax.ShapeDtypeStruct((B, S5_LANES), F32)
    return pl.pallas_call(
        functools.partial(_s5_body, B=B, Tc=Tc),
        grid=(T // Tc,),
        in_specs=[pl.BlockSpec((rows, S5_WIDTH), lambda i: (i, 0)),
                  pl.BlockSpec((B, S5_LANES), full2), pl.BlockSpec((B, S5_LANES), full2),
                  pl.BlockSpec((1, S5_LANES), full2), pl.BlockSpec((1, S5_LANES), full2),
                  pl.BlockSpec((S5_NBLK, LANE, S5_BLK_LANES), full3),
                  pl.BlockSpec((S5_NBLK, LANE, S5_BLK_LANES), full3),
                  pl.BlockSpec((S5_NBLK, S5_BLK_LANES, LANE), full3),
                  pl.BlockSpec((S5_NBLK, S5_BLK_LANES, LANE), full3),
                  pl.BlockSpec((1, S5_WIDTH), full2), pl.BlockSpec((S5_WIDTH, S5_WIDTH), full2),
                  pl.BlockSpec((1, S5_WIDTH), full2)],
        out_specs=(pl.BlockSpec((rows, S5_WIDTH), lambda i: (i, 0)),
                   pl.BlockSpec((B, S5_LANES), full2), pl.BlockSpec((B, S5_LANES), full2)),
        out_shape=(jax.ShapeDtypeStruct((T * B, S5_WIDTH), BF16), st, st),
        scratch_shapes=[pltpu.VMEM((B, S5_LANES), F32), pltpu.VMEM((B, S5_LANES), F32),
                        pltpu.VMEM((rows, S5_LANES), F32), pltpu.VMEM((rows, S5_LANES), F32)],
        compiler_params=_cparams(("arbitrary",)),
    )(u, h0r, h0i, lw['a_re'], lw['a_im'], lw['bb_re'], lw['bb_im'], lw['cc_re'], lw['cc_im'],
      lw['s5_d'], lw['w_glu'], lw['b_glu'])


def _out_body(x_ref, oa_ref, ob_ref, oc_ref, gt_ref, sc_ref, sh_ref, wo_ref, gffn_ref, wrh_ref, wrl_ref, br_ref,
              x1_ref, h2_ref, ti_ref, tg_ref):
    wa = MLA_HEADS * MLA_V
    o = (_dot(oa_ref[...], wo_ref[0:wa, :]) + _dot(ob_ref[...], wo_ref[wa:wa + S5_WIDTH, :])
         + _dot(oc_ref[...], wo_ref[wa + S5_WIDTH:, :]))
    x1 = x_ref[...] + gt_ref[0] * o
    x1_ref[...] = x1
    h2 = _rms(x1, gffn_ref[...]) * (1.0 + sc_ref[0]) + sh_ref[0]
    h2_ref[...] = h2
    hi, lo = _split_bf16(h2)
    logits = _dot(hi, wrh_ref[...]) + _dot(hi, wrl_ref[...]) + _dot(lo, wrh_ref[...]) + br_ref[...]
    lane = lax.broadcasted_iota(jnp.int32, logits.shape, 1)
    lanef = lane.astype(F32)
    vals, idxs = [], []
    for _ in range(TOP_K):
        m = jnp.max(logits, axis=-1, keepdims=True)
        idx = jnp.min(jnp.where(logits == m, lanef, float(ROUTER_PAD)), axis=-1, keepdims=True).astype(jnp.int32)
        vals.append(m)
        idxs.append(idx)
        logits = jnp.where(lane == idx, -jnp.inf, logits)
    es = [jnp.exp(v - vals[0]) for v in vals]
    den = es[0] + es[1] + es[2] + es[3]
    ti = jnp.zeros(lane.shape, jnp.int32)
    tg = jnp.zeros(lane.shape, F32)
    for kk in range(TOP_K):
        ti = jnp.where(lane == kk, idxs[kk], ti)
        tg = jnp.where(lane == kk, es[kk] / den, tg)
    ti_ref[...] = ti
    tg_ref[...] = tg


def _outproj(x, oa, ob, oc, gt, sc, sh, lw, *, B, T, tm):
    N = B * T
    nT = T // tm
    mt = gt.shape[1]
    row = lambda b, t: (b * nT + t, 0)
    full = lambda b, t: (0, 0)
    if mt == 1:
        mod_spec = pl.BlockSpec((1, 1, D_MODEL), lambda b, t: (b, 0, 0))
    else:
        mod_spec = pl.BlockSpec((1, tm, D_MODEL), lambda b, t: (b, t, 0))
    weights = [lw['w_out'], lw['g_ffn'], lw['wr_hi'], lw['wr_lo'], lw['b_router']]
    return pl.pallas_call(
        _out_body, grid=(B, nT),
        in_specs=[pl.BlockSpec((tm, D_MODEL), row), pl.BlockSpec((tm, MLA_HEADS * MLA_V), row),
                  pl.BlockSpec((tm, S5_WIDTH), lambda b, t: (t, b)), pl.BlockSpec((tm, SB_WIDTH), row),
                  mod_spec, mod_spec, mod_spec] + [pl.BlockSpec(w.shape, full) for w in weights],
        out_specs=(pl.BlockSpec((tm, D_MODEL), row), pl.BlockSpec((tm, D_MODEL), row),
                   pl.BlockSpec((tm, ROUTER_PAD), row), pl.BlockSpec((tm, ROUTER_PAD), row)),
        out_shape=(jax.ShapeDtypeStruct((N, D_MODEL), F32), jax.ShapeDtypeStruct((N, D_MODEL), F32),
                   jax.ShapeDtypeStruct((N, ROUTER_PAD), jnp.int32), jax.ShapeDtypeStruct((N, ROUTER_PAD), F32)),
        compiler_params=_cparams(("parallel", "parallel")),
    )(x, oa, ob, oc, gt, sc, sh, *weights)


def _moe_body(be_ref, nu_ref, x_ref, wg_ref, wu_ref, wd_ref, bg_ref, bu_ref, bd_ref, o_ref, wd16_ref):
    i = pl.program_id(0)
    used = i < nu_ref[0]

    @pl.when(used & ((i == 0) | (be_ref[i] != be_ref[jnp.maximum(i - 1, 0)])))
    def _():
        wd16_ref[...] = wd_ref[0, 0].astype(BF16)

    @pl.when(used)
    def _():
        x = x_ref[...].astype(BF16)
        g = jnp.minimum(_dot(x, wg_ref[0]) + bg_ref[0], SWIGLU_LIMIT)
        up = jnp.clip(_dot(x, wu_ref[0]) + bu_ref[0], -SWIGLU_LIMIT, SWIGLU_LIMIT)
        act = (up + 1.0) * (g * jax.nn.sigmoid(SWIGLU_ALPHA * g))
        o_ref[...] = _dot(act.astype(BF16), wd16_ref[...]) + bd_ref[0]

    @pl.when(i >= nu_ref[0])
    def _():
        o_ref[...] = jnp.zeros(o_ref.shape, F32)


def _moe(block_expert, n_used, xs, lw, *, bm, layer):
    n_slots = xs.shape[0]
    n_blocks = n_slots // bm
    wsel = lambda i, be, nu: (be[i], 0, 0)
    row = lambda i, be, nu: (i, 0)
    return pl.pallas_call(
        _moe_body,
        grid_spec=pltpu.PrefetchScalarGridSpec(
            num_scalar_prefetch=2, grid=(n_blocks,),
            in_specs=[pl.BlockSpec((bm, D_MODEL), row),
                      pl.BlockSpec((1, D_MODEL, D_FF), wsel), pl.BlockSpec((1, D_MODEL, D_FF), wsel),
                      pl.BlockSpec((1, 1, D_FF, D_MODEL), lambda i, be, nu: (layer, be[i], 0, 0)),
                      pl.BlockSpec((1, 1, D_FF), wsel), pl.BlockSpec((1, 1, D_FF), wsel),
                      pl.BlockSpec((1, 1, D_MODEL), wsel)],
            out_specs=pl.BlockSpec((bm, D_MODEL), row),
            scratch_shapes=[pltpu.VMEM((D_FF, D_MODEL), BF16)]),
        out_shape=jax.ShapeDtypeStruct((n_slots, D_MODEL), F32),
        compiler_params=_cparams(("arbitrary",)),
    )(block_expert, n_used, xs, lw['w_gate'], lw['w_up'], lw['w_down'], lw['b_gate'], lw['b_up'], lw['b_down'])


def _route(topi, bm):
    n = topi.shape[0]
    tk = n * TOP_K
    n_blocks = -(-tk // bm) + N_EXPERTS
    n_fill = n_blocks * bm - tk
    flat_e = topi[:, :TOP_K].reshape(tk)
    counts = jnp.sum((flat_e[:, None] == jnp.arange(N_EXPERTS, dtype=jnp.int32)[None, :]).astype(jnp.int32), axis=0)
    blocks_per_e = (counts + bm - 1) // bm
    block_end = jnp.cumsum(blocks_per_e)
    pads = blocks_per_e * bm - counts
    f = jnp.arange(n_fill, dtype=jnp.int32)
    f_e, f_r = f // bm, f % bm
    f_pads = jnp.concatenate([jnp.repeat(pads, bm), jnp.zeros((n_fill - N_EXPERTS * bm,), jnp.int32)])
    keys = jnp.concatenate([2 * flat_e, jnp.where(f_r < f_pads, 2 * f_e + 1, 2 * N_EXPERTS)])
    order = jnp.argsort(keys).astype(jnp.int32)
    slot_tok = jnp.where(order < tk, order // TOP_K, 0)
    dest = jnp.argsort(order).astype(jnp.int32)[:tk]
    blk = jnp.arange(n_blocks, dtype=jnp.int32)
    block_expert = jnp.minimum(jnp.sum((block_end[None, :] <= blk[:, None]).astype(jnp.int32), axis=1),
                               N_EXPERTS - 1)
    return slot_tok, dest, block_expert, block_end[-1:].astype(jnp.int32)


SC_WINDOW = 32


def _gather_rows(x, idx):
    n_idx = idx.shape[0]
    d = x.shape[1]
    info = plsc.get_sparse_core_info()
    nc, ns = info.num_cores, info.num_subcores
    per_w = n_idx // (nc * ns)
    if n_idx % (nc * ns * SC_WINDOW) != 0:
        return jnp.take(x, idx, axis=0)
    mesh = plsc.VectorSubcoreMesh(core_axis_name="c", subcore_axis_name="s")

    @functools.partial(
        pl.kernel, out_type=jax.ShapeDtypeStruct((n_idx, d), x.dtype), mesh=mesh,
        scratch_types=[pltpu.VMEM((SC_WINDOW,), jnp.int32), pltpu.VMEM((SC_WINDOW, d), x.dtype),
                       pltpu.SemaphoreType.DMA])
    def gather_kernel(x_hbm, i_hbm, o_hbm, idx_v, rows_v, sem):
        base = (lax.axis_index("s") * nc + lax.axis_index("c")) * per_w

        @pl.loop(0, per_w // SC_WINDOW)
        def _(j):
            off = base + j * SC_WINDOW
            pltpu.sync_copy(i_hbm.at[pl.ds(off, SC_WINDOW)], idx_v)
            pltpu.async_copy(x_hbm.at[idx_v], rows_v, sem).wait()
            pltpu.sync_copy(rows_v, o_hbm.at[pl.ds(off, SC_WINDOW)])

    return gather_kernel(x, idx)


def _combine_body(x_ref, y_ref, g_ref, gt_ref, o_ref):
    g = g_ref[...]
    y = y_ref[0] * g[:, 0:1]
    for kk in range(1, TOP_K):
        y = y + y_ref[kk] * g[:, kk:kk + 1]
    o_ref[...] = x_ref[...] + gt_ref[0] * y


def _combine(x1, yk, topg, gt, *, B, T, tm):
    nT = T // tm
    mt = gt.shape[1]
    row = lambda b, t: (b * nT + t, 0)
    if mt == 1:
        mod_spec = pl.BlockSpec((1, 1, D_MODEL), lambda b, t: (b, 0, 0))
    else:
        mod_spec = pl.BlockSpec((1, tm, D_MODEL), lambda b, t: (b, t, 0))
    return pl.pallas_call(
        _combine_body, grid=(B, nT),
        in_specs=[pl.BlockSpec((tm, D_MODEL), row),
                  pl.BlockSpec((TOP_K, tm, D_MODEL), lambda b, t: (0, b * nT + t, 0)),
                  pl.BlockSpec((tm, ROUTER_PAD), row), mod_spec],
        out_specs=pl.BlockSpec((tm, D_MODEL), row),
        out_shape=jax.ShapeDtypeStruct(x1.shape, F32),
        compiler_params=_cparams(("parallel", "parallel")),
    )(x1, yk, topg, gt)


def _page_copies(pt_ref, b, c, slot, srcs, bufs, sems, keys_on_lanes, *, layer, G):
    out = []
    for g in range(G):
        page = pt_ref[b, c * G + g]
        for src, buf, sem, on_lanes in zip(srcs, bufs, sems, keys_on_lanes):
            keys = pl.ds(g * PAGE_SIZE, PAGE_SIZE)
            dst = buf.at[slot, :, keys] if on_lanes else buf.at[slot, keys]
            out.append(pltpu.make_async_copy(src.at[layer, page], dst, sem.at[slot]))
    return out


def _mla_dec_body(pt_ref, qm_ref, qpe_ref, q8_ref, k8_ref, cn_ref, wukt_ref, wuv_ref, kv_hbm, kr_hbm,
                  o_ref, kvbuf, krbuf, sem_kv, sem_kr, m_ref, l_ref, pc_ref, lhs_ref, *, layer, G, nc, nb, scale):
    b = pl.program_id(0)
    c = pl.program_id(1)
    step = b * nc + c
    slot = lax.rem(step, 2)
    copies = functools.partial(_page_copies, pt_ref, srcs=(kv_hbm, kr_hbm), bufs=(kvbuf, krbuf),
                               sems=(sem_kv, sem_kr), keys_on_lanes=(False, True), layer=layer, G=G)

    @pl.when(step == 0)
    def _():
        for cp in copies(0, 0, 0):
            cp.start()

    @pl.when(step + 1 < nb * nc)
    def _():
        nxt = step + 1
        for cp in copies(nxt // nc, lax.rem(nxt, nc), 1 - slot):
            cp.start()

    @pl.when(c == 0)
    def _():
        m_ref[...] = jnp.full(m_ref.shape, -jnp.inf, F32)
        l_ref[...] = jnp.zeros(l_ref.shape, F32)
        pc_ref[...] = jnp.zeros(pc_ref.shape, F32)
        nk = MLA_HEADS * MLA_NOPE
        lhs_ref[0:nk, :] = wukt_ref[...]
        qabs = _dot(qm_ref[0], wukt_ref[...])
        hi = qabs.astype(BF16).astype(F32)
        lhs_ref[nk:nk + 16, :] = jnp.concatenate([hi, qabs - hi], axis=0).astype(BF16)

    for cp in copies(b, c, slot):
        cp.wait()

    nk = MLA_HEADS * MLA_NOPE
    cb = kvbuf[slot].astype(BF16)
    res = _dot_nt(lhs_ref[...], cb)
    sn = res[nk:nk + 8, :] + res[nk + 8:nk + 16, :]
    ss = []
    for h in range(MLA_HEADS):
        kh = res[h * MLA_NOPE:(h + 1) * MLA_NOPE, :]
        ss.append(jnp.sum(kh * kh, axis=0, keepdims=True))
    ss = jnp.concatenate(ss + [jnp.zeros((8 - MLA_HEADS, res.shape[1]), F32)], axis=0)
    s = sn * lax.rsqrt(ss * (1.0 / MLA_NOPE) + EPS)
    s = (s + _dot(qpe_ref[0], krbuf[slot].astype(BF16))) * scale
    m_prev = m_ref[...]
    m_new = jnp.maximum(m_prev, jnp.max(s, axis=-1, keepdims=True))
    alpha = jnp.exp(m_prev - m_new)
    p = jnp.exp(s - m_new[:, :1])
    l_ref[...] = alpha * l_ref[...] + jnp.sum(p, axis=-1, keepdims=True)
    pc_ref[...] = alpha * pc_ref[...] + _dot(p.astype(BF16), cb)
    m_ref[...] = m_new

    @pl.when(c == nc - 1)
    def _():
        s_new = jnp.sum(q8_ref[0].astype(F32) * k8_ref[0].astype(F32), axis=-1, keepdims=True) * scale
        m_f = jnp.maximum(m_ref[...], s_new)
        a_old = jnp.exp(m_ref[...] - m_f)
        p_new = jnp.exp(s_new - m_f)
        l_f = a_old * l_ref[...] + p_new
        ctx = (a_old * pc_ref[...] + p_new * cn_ref[0]) / l_f
        full = _dot(ctx.astype(BF16), wuv_ref[...])
        rowi = lax.broadcasted_iota(jnp.int32, full.shape, 0)
        coli = lax.broadcasted_iota(jnp.int32, full.shape, 1)
        own = (coli >= rowi * MLA_V) & (coli < (rowi + 1) * MLA_V)
        o_ref[0] = jnp.sum(jnp.where(own, full, 0.0), axis=0, keepdims=True).astype(o_ref.dtype)


def _mla_decode(page_table, qm, qpe, q8, k8, cn, lw, cache_kv, cache_krt, *, layer):
    nb, n_pages = page_table.shape
    G = min(32, n_pages)
    nc = n_pages // G
    rows = G * PAGE_SIZE
    scale = (MLA_NOPE + MLA_ROPE) ** -0.5
    per_b = lambda b, c, pt: (b, 0, 0)
    full = lambda b, c, pt: (0, 0)
    return pl.pallas_call(
        functools.partial(_mla_dec_body, layer=layer, G=G, nc=nc, nb=nb, scale=scale),
        grid_spec=pltpu.PrefetchScalarGridSpec(
            num_scalar_prefetch=1, grid=(nb, nc),
            in_specs=[pl.BlockSpec((1, 8, MLA_HEADS * MLA_NOPE), per_b), pl.BlockSpec((1, 8, MLA_ROPE), per_b),
                      pl.BlockSpec((1, 8, HEAD_PAD), per_b), pl.BlockSpec((1, 8, HEAD_PAD), per_b),
                      pl.BlockSpec((1, 1, MLA_KV_LORA), per_b),
                      pl.BlockSpec(lw['wukt'].shape, full), pl.BlockSpec(lw['wuv'].shape, full),
                      pl.BlockSpec(memory_space=pl.ANY), pl.BlockSpec(memory_space=pl.ANY)],
            out_specs=pl.BlockSpec((1, 1, MLA_HEADS * MLA_V), per_b),
            scratch_shapes=[pltpu.VMEM((2, rows, MLA_KV_LORA), F32), pltpu.VMEM((2, MLA_ROPE, rows), F32),
                            pltpu.SemaphoreType.DMA((2,)), pltpu.SemaphoreType.DMA((2,)),
                            pltpu.VMEM((8, LANE), F32), pltpu.VMEM((8, LANE), F32),
                            pltpu.VMEM((8, MLA_KV_LORA), F32),
                            pltpu.VMEM((MLA_HEADS * MLA_NOPE + 16, MLA_KV_LORA), BF16)]),
        out_shape=jax.ShapeDtypeStruct((nb, 1, MLA_HEADS * MLA_V), BF16),
        compiler_params=_cparams(("arbitrary", "arbitrary")),
    )(page_table, qm, qpe, q8, k8, cn, lw['wukt'], lw['wuv'], cache_kv, cache_krt)


def _sb_dec_body(pt_ref, q_ref, uo_ref, sfx_ref, k_hbm, v_hbm, o_ref, kbuf, vbuf, sem_k, sem_v, acc_ref, c_ref,
                 *, layer, G, nc, nb):
    b = pl.program_id(0)
    ci = pl.program_id(1)
    step = b * nc + ci
    slot = lax.rem(step, 2)
    copies = functools.partial(_page_copies, pt_ref, srcs=(k_hbm, v_hbm), bufs=(kbuf, vbuf),
                               sems=(sem_k, sem_v), keys_on_lanes=(True, True), layer=layer, G=G)

    @pl.when(step == 0)
    def _():
        for cp in copies(0, nc - 1, 0):
            cp.start()

    @pl.when(step + 1 < nb * nc)
    def _():
        nxt = step + 1
        for cp in copies(nxt // nc, nc - 1 - lax.rem(nxt, nc), 1 - slot):
            cp.start()

    @pl.when(ci == 0)
    def _():
        acc_ref[...] = jnp.zeros(acc_ref.shape, F32)
        c_ref[...] = jnp.zeros(c_ref.shape, F32)

    for cp in copies(b, nc - 1 - ci, slot):
        cp.wait()

    q = q_ref[0]
    kt = kbuf[slot].astype(BF16)
    vt = vbuf[slot].astype(BF16)
    page = lambda a, g: a[:, g * PAGE_SIZE:(g + 1) * PAGE_SIZE]
    z = jnp.concatenate([_dot(q, page(kt, g)) for g in range(G)], axis=0)
    lb, lk = _log_sigmoid_pair(z)
    hi, lo = _split_bf16(lk)
    r = _dot(hi, uo_ref[...]) + _dot(lo, uo_ref[...])
    th, tl = _split_bf16(r[:, PAGE_SIZE:])
    later = _dot(sfx_ref[...], th) + _dot(sfx_ref[...], tl)
    c = c_ref[...]
    w = jnp.exp(lb + r[:, :PAGE_SIZE] + later + jnp.concatenate([c] * G, axis=0))
    acc = acc_ref[...]
    for g in range(G):
        acc = acc + _dot_nt(w[g * 8:(g + 1) * 8, :].astype(BF16), page(vt, g))
    acc_ref[...] = acc
    c_ref[...] = c + later[0:8, :] + r[0:8, PAGE_SIZE:]

    @pl.when(ci == nc - 1)
    def _():
        acc = acc_ref[...]
        rowi = lax.broadcasted_iota(jnp.int32, acc.shape, 0)
        coli = lax.broadcasted_iota(jnp.int32, acc.shape, 1)
        own = (coli >= rowi * SB_HEAD_DIM) & (coli < (rowi + 1) * SB_HEAD_DIM)
        o_ref[0] = jnp.sum(jnp.where(own, acc, 0.0), axis=0, keepdims=True).astype(o_ref.dtype)


def _sb_decode(page_table, qs, cache_kt, cache_vt, *, layer):
    nb, n_pages = page_table.shape
    G = min(16, n_pages)
    nc = n_pages // G
    rows = G * PAGE_SIZE
    r = np.arange(8 * G)
    sfx = jnp.asarray((r[None, :] % 8 == r[:, None] % 8) & (r[None, :] // 8 > r[:, None] // 8), BF16)
    per_b = lambda b, c, pt: (b, 0, 0)
    full = lambda b, c, pt: (0, 0)
    return pl.pallas_call(
        functools.partial(_sb_dec_body, layer=layer, G=G, nc=nc, nb=nb),
        grid_spec=pltpu.PrefetchScalarGridSpec(
            num_scalar_prefetch=1, grid=(nb, nc),
            in_specs=[pl.BlockSpec((1, 8, SB_WIDTH), per_b), pl.BlockSpec((PAGE_SIZE, 2 * PAGE_SIZE), full),
                      pl.BlockSpec((8 * G, 8 * G), full),
                      pl.BlockSpec(memory_space=pl.ANY), pl.BlockSpec(memory_space=pl.ANY)],
            out_specs=pl.BlockSpec((1, 1, SB_WIDTH), per_b),
            scratch_shapes=[pltpu.VMEM((2, SB_WIDTH, rows), F32), pltpu.VMEM((2, SB_WIDTH, rows), F32),
                            pltpu.SemaphoreType.DMA((2,)), pltpu.SemaphoreType.DMA((2,)),
                            pltpu.VMEM((8, SB_WIDTH), F32), pltpu.VMEM((8, LANE), F32)]),
        out_shape=jax.ShapeDtypeStruct((nb, 1, SB_WIDTH), BF16),
        compiler_params=_cparams(("arbitrary", "arbitrary")),
    )(page_table, qs, _suffix_ones(PAGE_SIZE), sfx, cache_kt, cache_vt)


def _rope_tables(pos):
    half = MLA_ROPE // 2
    inv = ROPE_THETA ** (-jnp.arange(half, dtype=F32) / half)
    ang = pos.astype(F32)[:, None] * inv
    cos, sin = jnp.cos(ang), jnp.sin(ang)
    n = pos.shape[0]
    z16 = jnp.zeros((n, half), F32)
    z32 = jnp.zeros((n, HEAD_PAD - MLA_NOPE - MLA_ROPE), F32)
    z64 = jnp.zeros((n, MLA_NOPE), F32)
    cos_t = jnp.concatenate([jnp.ones((n, MLA_NOPE), F32), cos, cos, z32], axis=1)
    sin_a = jnp.concatenate([z64, -sin, z16, z32], axis=1)
    sin_b = jnp.concatenate([z64, z16, sin, z32], axis=1)
    return cos_t, sin_a, sin_b


def _pad_heads(w, width):
    K = w.shape[0]
    w = w.reshape(K, MLA_HEADS, width)
    w = jnp.pad(w, ((0, 0), (0, 0), (0, HEAD_PAD - width)))
    return w.reshape(K, MLA_HEADS * HEAD_PAD)


def _block_diag(w):
    a, b = w.shape[1], w.shape[2]
    w = w.reshape(S5_NBLK, S5_BLK_GROUPS, a, b)
    eye = jnp.eye(S5_BLK_GROUPS, dtype=w.dtype)
    out = jnp.einsum('jgab,gh->jgahb', w, eye)
    return out.reshape(S5_NBLK, S5_BLK_GROUPS * a, S5_BLK_GROUPS * b)


def _layer_weights(l, p, disc):
    a_re, a_im, f_re, f_im = [d[l] for d in disc]
    lw = {}
    w_in = p['w_in'][l]
    kpe_cols = jnp.pad(w_in[:, 384:416], ((0, 0), (MLA_NOPE, HEAD_PAD - MLA_NOPE - MLA_ROPE)))
    lw['w1'] = jnp.concatenate([w_in[:, 0:384], kpe_cols, w_in[:, 416:]], axis=1).astype(BF16)
    row = lambda v: v.reshape(1, -1).astype(F32)
    lw['g_mix'] = row(p['g_mix'][l])
    lw['g_q_lat'] = row(p['g_q_lat'][l])
    lw['g_kv_lat'] = row(p['g_kv_lat'][l])
    lw['wuq'] = _pad_heads(p['w_uq'][l], MLA_NOPE + MLA_ROPE).astype(BF16)
    lw['wuk'] = _pad_heads(p['w_uk'][l], MLA_NOPE).astype(BF16)
    lw['wukt'] = p['w_uk'][l].T.astype(BF16)
    lw['wuv'] = p['w_uv'][l].astype(BF16)
    zpad = jnp.zeros((HEAD_PAD - MLA_NOPE - MLA_ROPE,), F32)
    lw['gq'] = row(jnp.concatenate([p['g_qn_nope'][l], p['g_qn_rope'][l], zpad]))
    lw['gk'] = row(jnp.concatenate([p['g_kn_nope'][l], jnp.zeros((HEAD_PAD - MLA_NOPE,), F32)]))
    lw['gkr'] = row(jnp.concatenate([jnp.zeros((MLA_NOPE,), F32), p['g_kn_rope'][l], zpad]))
    br, bi = p['s5_b_re'][l], p['s5_b_im'][l]
    bb_re = f_re[..., None] * br - f_im[..., None] * bi
    bb_im = f_re[..., None] * bi + f_im[..., None] * br
    lw['bb_re'] = _block_diag(jnp.swapaxes(bb_re, 1, 2)).astype(BF16)
    lw['bb_im'] = _block_diag(jnp.swapaxes(bb_im, 1, 2)).astype(BF16)
    lw['cc_re'] = _block_diag(jnp.swapaxes(p['s5_c_re'][l], 1, 2)).astype(BF16)
    lw['cc_im'] = _block_diag(jnp.swapaxes(p['s5_c_im'][l], 1, 2)).astype(BF16)
    lw['a_re'] = a_re.reshape(1, S5_LANES)
    lw['a_im'] = a_im.reshape(1, S5_LANES)
    lw['s5_d'] = row(p['s5_d'][l])
    lw['w_glu'] = p['w_glu'][l].astype(BF16)
    lw['b_glu'] = row(p['b_glu'][l])
    lw['w_out'] = p['w_out'][l].astype(BF16)
    lw['g_ffn'] = row(p['g_ffn'][l])
    wr = jnp.pad(p['w_router'][l], ((0, 0), (0, ROUTER_PAD - N_EXPERTS)))
    lw['wr_hi'] = wr.astype(BF16)
    lw['wr_lo'] = (wr - lw['wr_hi'].astype(F32)).astype(BF16)
    lw['b_router'] = row(jnp.concatenate([p['b_router'][l], jnp.full((ROUTER_PAD - N_EXPERTS,), NEG_BIG, F32)]))
    w_gu = p['w_gu'][l].reshape(N_EXPERTS, D_MODEL, D_FF, 2)
    lw['w_gate'] = w_gu[..., 0].astype(BF16)
    lw['w_up'] = w_gu[..., 1].astype(BF16)
    lw['w_down'] = p['w_down']
    b_gu = p['b_gu'][l].reshape(N_EXPERTS, 1, D_FF, 2)
    lw['b_gate'] = b_gu[..., 0]
    lw['b_up'] = b_gu[..., 1]
    lw['b_down'] = p['b_down'][l].reshape(N_EXPERTS, 1, D_MODEL)
    return lw


def _split_mod(mod):
    return [mod[..., i * D_MODEL:(i + 1) * D_MODEL] for i in range(N_MOD)]


def _ffn(x1, h2, topi, topg, gt_f, lw, *, B, T, tm, bm, layer):
    n = B * T
    slot_tok, dest, block_expert, n_used = _route(topi, bm)
    xs = _gather_rows(h2, slot_tok)
    ys = _moe(block_expert, n_used, xs, lw, bm=bm, layer=layer)
    yk = _gather_rows(ys, dest.reshape(n, TOP_K).T.reshape(n * TOP_K)).reshape(TOP_K, n, D_MODEL)
    return _combine(x1, yk, topg, gt_f, B=B, T=T, tm=512 if T % 512 == 0 else tm)


def _prompt_layer(x, mods, tabs, lw, *, B, T, l):
    sh_a, sc_a, gt_a, sh_f, sc_f, gt_f = mods
    tm = min(256, T)
    q, k, v, ckv, kpe, u, sq, sk16, sv16, sk, sv = _proj(x, sc_a, sh_a, tabs, lw, B=B, T=T, tm=tm, qk_dtype=BF16)
    o_a = _mla_prompt(q, k, v, B=B, T=T, tq=512 if T % 512 == 0 else min(256, T))
    o_c = _sb_prompt(sq, sk16, sv16, B=B, T=T, tq=min(256, T))
    zeros = jnp.zeros((B, S5_LANES), F32)
    Tc = min(64, T)
    o_b, s_re, s_im = _s5(u.reshape(T * B, S5_WIDTH), zeros, zeros, lw, B=B, T=T, Tc=Tc)
    x1, h2, topi, topg = _outproj(x, o_a, o_b.reshape(T, B * S5_WIDTH), o_c, gt_a, sc_f, sh_f, lw, B=B, T=T, tm=tm)
    x2 = _ffn(x1, h2, topi, topg, gt_f, lw, B=B, T=T, tm=tm, bm=min(256, max(16, (B * T * TOP_K) // N_EXPERTS)),
              layer=l)
    state = (ckv.reshape(B, T, MLA_KV_LORA), kpe.reshape(B, T, MLA_ROPE),
             sk.reshape(B, T, SB_HEADS, SB_HEAD_DIM), sv.reshape(B, T, SB_HEADS, SB_HEAD_DIM),
             s_re.reshape(B, S5_GROUPS, S5_STATE), s_im.reshape(B, S5_GROUPS, S5_STATE))
    return x2, state


def _sample_layer(x, mods, tabs, lw, caches, h0, page_table, p, *, l, DB):
    sh_a, sc_a, gt_a, sh_f, sc_f, gt_f = mods
    cache_kv, cache_kr, cache_sk, cache_sv = caches
    q, k, v, ckv, kpe, u, sq, sk16, sv16, sk, sv = _proj(x, sc_a, sh_a, tabs, lw, B=1, T=DB, tm=DB, qk_dtype=F32)
    q8 = jnp.pad(q.reshape(DB, MLA_HEADS, HEAD_PAD), ((0, 0), (0, 8 - MLA_HEADS), (0, 0)))
    k8 = jnp.pad(k.reshape(DB, MLA_HEADS, HEAD_PAD), ((0, 0), (0, 8 - MLA_HEADS), (0, 0)))
    qg = q8[:, :, :MLA_NOPE] * p['g_kn_nope'][l][None, None, :]
    qm = jnp.einsum('bhd,hg->bhgd', qg, jnp.eye(8, MLA_HEADS, dtype=F32)).reshape(DB, 8, MLA_HEADS * MLA_NOPE)
    qpe = q8[:, :, MLA_NOPE:MLA_NOPE + MLA_ROPE].astype(BF16)
    o_a = _mla_decode(page_table, qm.astype(BF16), qpe, q8.astype(BF16), k8.astype(BF16),
                      ckv.reshape(DB, 1, MLA_KV_LORA),
                      lw, cache_kv, cache_kr, layer=l).reshape(DB, MLA_HEADS * MLA_V)
    sq4 = sq.reshape(DB, SB_HEADS, SB_HEAD_DIM)
    eye4 = jnp.eye(8, SB_HEADS, dtype=BF16)
    qs = jnp.einsum('bhd,rh,hg->brgd', sq4, eye4, jnp.eye(SB_HEADS, dtype=BF16)).reshape(DB, 8, SB_WIDTH)
    o_c = _sb_decode(page_table, qs, cache_sk, cache_sv, layer=l).reshape(DB, SB_WIDTH)
    o_b, s_re, s_im = _s5(u, h0[0], h0[1], lw, B=DB, T=1, Tc=1)
    x1, h2, topi, topg = _outproj(x, o_a, o_b, o_c, gt_a, sc_f, sh_f, lw, B=1, T=DB, tm=DB)
    x2 = _ffn(x1, h2, topi, topg, gt_f, lw, B=1, T=DB, tm=DB, bm=16, layer=l)
    state = (ckv.reshape(DB, 1, MLA_KV_LORA), kpe.reshape(DB, 1, MLA_ROPE),
             sk.reshape(DB, 1, SB_HEADS, SB_HEAD_DIM), sv.reshape(DB, 1, SB_HEADS, SB_HEAD_DIM),
             s_re.reshape(DB, S5_GROUPS, S5_STATE), s_im.reshape(DB, S5_GROUPS, S5_STATE))
    return x2, state


def kernel(x_prompt, x_sample, cache_kv_latent, cache_k_rope, cache_sb_k, cache_sb_v, state_s5_re, state_s5_im, page_table, c_prompt, c_sample, w_ada, b_ada, g_mix, w_in, g_q_lat, w_uq, g_qn_nope, g_qn_rope, g_kv_lat, g_kn_rope, w_uk, w_uv, g_kn_nope, s5_lam_re, s5_lam_im, s5_log_step, s5_b_re, s5_b_im, s5_c_re, s5_c_im, s5_d, w_glu, b_glu, w_out, g_ffn, w_router, b_router, w_gu, b_gu, w_down, b_down):
    B, T, _ = x_prompt.shape
    DB, DT, _ = x_sample.shape
    assert DT == 1, "the sample group carries one new token per row"
    L = w_ada.shape[0]
    n_pool = cache_kv_latent.shape[1]
    past_len = page_table.shape[1] * PAGE_SIZE
    p = dict(w_in=w_in, g_mix=g_mix, g_q_lat=g_q_lat, w_uq=w_uq, g_qn_nope=g_qn_nope, g_qn_rope=g_qn_rope,
             g_kv_lat=g_kv_lat, g_kn_rope=g_kn_rope, w_uk=w_uk, w_uv=w_uv, g_kn_nope=g_kn_nope,
             s5_b_re=s5_b_re, s5_b_im=s5_b_im, s5_c_re=s5_c_re, s5_c_im=s5_c_im, s5_d=s5_d, w_glu=w_glu,
             b_glu=b_glu, w_out=w_out, g_ffn=g_ffn, w_router=w_router, b_router=b_router, w_gu=w_gu, b_gu=b_gu,
             w_down=w_down, b_down=b_down)
    mod = _ada(jnp.concatenate([c_prompt, c_sample], axis=0), w_ada, b_ada)
    disc = _s5disc(s5_lam_re, s5_lam_im, s5_log_step)
    tabs_p = _rope_tables(jnp.arange(T, dtype=jnp.int32))
    tabs_s = _rope_tables(jnp.full((DB,), past_len, jnp.int32))
    caches = (cache_kv_latent, jnp.transpose(cache_k_rope, (0, 1, 3, 2)),
              jnp.transpose(cache_sb_k, (0, 1, 3, 4, 2)).reshape(L, n_pool, SB_WIDTH, PAGE_SIZE),
              jnp.transpose(cache_sb_v, (0, 1, 3, 4, 2)).reshape(L, n_pool, SB_WIDTH, PAGE_SIZE))
    xp = x_prompt.reshape(B * T, D_MODEL)
    xs = x_sample.reshape(DB, D_MODEL)
    p_rows, s_rows = [], []
    for l in range(L):
        lw = _layer_weights(l, p, disc)
        mods_p = _split_mod(mod[l, :B].reshape(B, 1, N_MOD * D_MODEL))
        mods_s = _split_mod(mod[l, B:].reshape(1, DB, N_MOD * D_MODEL))
        xp, st_p = _prompt_layer(xp, mods_p, tabs_p, lw, B=B, T=T, l=l)
        p_rows.append(st_p)
        h0 = (state_s5_re[l].reshape(DB, S5_LANES), state_s5_im[l].reshape(DB, S5_LANES))
        xs, st_s = _sample_layer(xs, mods_s, tabs_s, lw, caches, h0, page_table, p, l=l, DB=DB)
        s_rows.append(st_s)
    p_out = [jnp.stack(a) for a in zip(*p_rows)]
    s_out = [jnp.stack(a) for a in zip(*s_rows)]
    return (xp.reshape(B, T, D_MODEL), xs.reshape(DB, 1, D_MODEL), *p_out, *s_out)
```

```python
import functools

import jax
import jax.numpy as jnp
import numpy as np
from jax import lax
from jax.experimental import pallas as pl
from jax.experimental.pallas import tpu as pltpu
from jax.experimental.pallas import tpu_sc as plsc

F32 = jnp.float32
BF16 = jnp.bfloat16

D_MODEL = 1024
PAGE_SIZE = 128
MLA_HEADS = 6
MLA_NOPE = 64
MLA_ROPE = 32
MLA_V = 64
MLA_Q_LORA = 256
MLA_KV_LORA = 128
ROPE_THETA = 10000.0
S5_CH = 16
S5_GROUPS = 24
S5_STATE = 64
S5_WIDTH = S5_GROUPS * S5_CH
S5_LANES = S5_GROUPS * S5_STATE
SB_HEADS = 4
SB_HEAD_DIM = 64
SB_WIDTH = SB_HEADS * SB_HEAD_DIM
N_EXPERTS = 32
TOP_K = 4
D_FF = 1024
SWIGLU_LIMIT = 7.0
SWIGLU_ALPHA = 1.702
N_MOD = 6
EPS = 1e-6

LANE = 128
HEAD_PAD = 128
S5_BLK_GROUPS = LANE // S5_CH
S5_NBLK = S5_GROUPS // S5_BLK_GROUPS
S5_BLK_LANES = S5_BLK_GROUPS * S5_STATE
ROUTER_PAD = 128
NEG_BIG = -1e30
VMEM_LIMIT = 56 * 1024 * 1024

_C_Q, _C_KV, _C_KPE, _C_U, _C_SQ, _C_SK, _C_SV, _C_END = 0, 256, 384, 512, 896, 1152, 1408, 1664


def _cparams(sem):
    return pltpu.CompilerParams(dimension_semantics=sem, vmem_limit_bytes=VMEM_LIMIT)


def _dot(a, b):
    return jnp.dot(a, b, preferred_element_type=F32)


def _dot_nt(a, b):
    return lax.dot_general(a, b, (((1,), (1,)), ((), ())), preferred_element_type=F32)


def _rms(x, g):
    return x * lax.rsqrt(jnp.mean(x * x, axis=-1, keepdims=True) + EPS) * g


def _rope_lanes(x, cos, sin_a, sin_b):
    return x * cos + pltpu.roll(x, HEAD_PAD - MLA_ROPE // 2, 1) * sin_a + pltpu.roll(x, MLA_ROPE // 2, 1) * sin_b


def _ada_body(c_ref, w_ref, b_ref, o_ref):
    c = c_ref[...]
    s = (c * jax.nn.sigmoid(c)).astype(BF16)
    o_ref[0] = _dot(s, w_ref[0].astype(BF16)) + b_ref[0]


def _ada(c_all, w_ada, b_ada):
    L = w_ada.shape[0]
    R = c_all.shape[0]
    ncol = N_MOD * D_MODEL // D_MODEL
    return pl.pallas_call(
        _ada_body,
        grid=(L, ncol),
        in_specs=[pl.BlockSpec((R, D_MODEL), lambda l, j: (0, 0)),
                  pl.BlockSpec((1, D_MODEL, D_MODEL), lambda l, j: (l, 0, j)),
                  pl.BlockSpec((1, 1, D_MODEL), lambda l, j: (l, 0, j))],
        out_specs=pl.BlockSpec((1, R, D_MODEL), lambda l, j: (l, 0, j)),
        out_shape=jax.ShapeDtypeStruct((L, R, N_MOD * D_MODEL), F32),
        compiler_params=_cparams(("parallel", "parallel")),
    )(c_all, w_ada, b_ada.reshape(L, 1, N_MOD * D_MODEL))


def _s5disc_body(lr_ref, li_ref, ls_ref, ar_ref, ai_ref, fr_ref, fi_ref):
    lr, li = lr_ref[...], li_ref[...]
    dt = jnp.exp(ls_ref[...])
    mag = jnp.exp(lr * dt)
    a_re, a_im = mag * jnp.cos(li * dt), mag * jnp.sin(li * dt)
    nr, ni = a_re - 1.0, a_im
    den = lr * lr + li * li
    ar_ref[...] = a_re
    ai_ref[...] = a_im
    fr_ref[...] = (nr * lr + ni * li) / den
    fi_ref[...] = (ni * lr - nr * li) / den


def _s5disc(lam_re, lam_im, log_step):
    L, G, N = lam_re.shape
    R = L * G
    sds = jax.ShapeDtypeStruct((R, N), F32)
    outs = pl.pallas_call(_s5disc_body, out_shape=(sds, sds, sds, sds))(
        lam_re.reshape(R, N), lam_im.reshape(R, N), log_step.reshape(R, 1))
    return [o.reshape(L, G, N) for o in outs]


def _proj_body(x_ref, sc_ref, sh_ref, gmix_ref, w1_ref, gql_ref, wuq_ref, gkv_ref, wuk_ref, wuv_ref,
               gq_ref, gk_ref, gkr_ref, cos_ref, sa_ref, sb_ref,
               q_ref, k_ref, v_ref, ckv_ref, kpe_ref, u_ref, sq_ref, sk16_ref, sv16_ref, sk_ref, sv_ref):
    x = x_ref[...]
    hn = _rms(x, gmix_ref[...]) * (1.0 + sc_ref[0]) + sh_ref[0]
    z = _dot(hn.astype(BF16), w1_ref[...])
    tm = x.shape[0]
    lane = lax.broadcasted_iota(jnp.int32, (tm, HEAD_PAD), 1)
    nope = lane < MLA_NOPE
    cos, sin_a, sin_b = cos_ref[...], sa_ref[...], sb_ref[...]

    ckv = _rms(z[:, _C_KV:_C_KPE], gkv_ref[...])
    ckv_ref[...] = ckv
    kraw = z[:, _C_KPE:_C_U]
    kr = kraw * lax.rsqrt(jnp.sum(kraw * kraw, axis=-1, keepdims=True) * (1.0 / MLA_ROPE) + EPS) * gkr_ref[...]
    kr = _rope_lanes(kr, cos, sin_a, sin_b)
    kpe_ref[...] = pltpu.roll(kr, HEAD_PAD - MLA_NOPE, 1)[:, :MLA_ROPE]

    qn = _rms(z[:, _C_Q:_C_KV], gql_ref[...]).astype(BF16)
    qa = _dot(qn, wuq_ref[...])
    cb = ckv.astype(BF16)
    ka = _dot(cb, wuk_ref[...])
    for h in range(MLA_HEADS):
        sl = slice(h * HEAD_PAD, (h + 1) * HEAD_PAD)
        qh = qa[:, sl]
        q2 = qh * qh
        ss_n = jnp.sum(jnp.where(nope, q2, 0.0), axis=-1, keepdims=True) * (1.0 / MLA_NOPE)
        ss_r = jnp.sum(jnp.where(nope, 0.0, q2), axis=-1, keepdims=True) * (1.0 / MLA_ROPE)
        qh = qh * jnp.where(nope, lax.rsqrt(ss_n + EPS), lax.rsqrt(ss_r + EPS)) * gq_ref[...]
        q_ref[:, sl] = _rope_lanes(qh, cos, sin_a, sin_b).astype(q_ref.dtype)
        kh = ka[:, sl]
        kh = kh * lax.rsqrt(jnp.sum(kh * kh, axis=-1, keepdims=True) * (1.0 / MLA_NOPE) + EPS) * gk_ref[...]
        k_ref[:, sl] = (kh + kr).astype(k_ref.dtype)
    v_ref[...] = _dot(cb, wuv_ref[...]).astype(v_ref.dtype)

    u_ref[...] = z[:, _C_U:_C_SQ]
    sq_ref[...] = (z[:, _C_SQ:_C_SK] * (SB_HEAD_DIM ** -0.5)).astype(BF16)
    sk = z[:, _C_SK:_C_SV]
    sv = z[:, _C_SV:_C_END]
    sk_ref[...] = sk
    sv_ref[...] = sv
    sk16_ref[...] = sk.astype(BF16)
    sv16_ref[...] = sv.astype(BF16)


def _proj(x, sc, sh, tabs, lw, *, B, T, tm, qk_dtype):
    N = B * T
    nT = T // tm
    mt = sc.shape[1]
    row = lambda b, t: (b * nT + t, 0)
    full = lambda b, t: (0, 0)
    if mt == 1:
        mod_spec = pl.BlockSpec((1, 1, D_MODEL), lambda b, t: (b, 0, 0))
    else:
        mod_spec = pl.BlockSpec((1, tm, D_MODEL), lambda b, t: (b, t, 0))
    tab_spec = pl.BlockSpec((tm, HEAD_PAD), lambda b, t: (t, 0))

    def wspec(a):
        return pl.BlockSpec(a.shape, full)

    weights = [lw['g_mix'], lw['w1'], lw['g_q_lat'], lw['wuq'], lw['g_kv_lat'], lw['wuk'], lw['wuv'],
               lw['gq'], lw['gk'], lw['gkr']]
    in_specs = ([pl.BlockSpec((tm, D_MODEL), row), mod_spec, mod_spec] + [wspec(w) for w in weights[:1]]
                + [wspec(w) for w in weights[1:]] + [tab_spec] * 3)
    QW = MLA_HEADS * HEAD_PAD
    VW = MLA_HEADS * MLA_V
    out_shape = (
        jax.ShapeDtypeStruct((N, QW), qk_dtype), jax.ShapeDtypeStruct((N, QW), qk_dtype),
        jax.ShapeDtypeStruct((N, VW), BF16),
        jax.ShapeDtypeStruct((N, MLA_KV_LORA), F32), jax.ShapeDtypeStruct((N, MLA_ROPE), F32),
        jax.ShapeDtypeStruct((T, B * S5_WIDTH), F32),
        jax.ShapeDtypeStruct((N, SB_WIDTH), BF16), jax.ShapeDtypeStruct((N, SB_WIDTH), BF16),
        jax.ShapeDtypeStruct((N, SB_WIDTH), BF16),
        jax.ShapeDtypeStruct((N, SB_WIDTH), F32), jax.ShapeDtypeStruct((N, SB_WIDTH), F32))
    out_specs = (
        pl.BlockSpec((tm, QW), row), pl.BlockSpec((tm, QW), row), pl.BlockSpec((tm, VW), row),
        pl.BlockSpec((tm, MLA_KV_LORA), row), pl.BlockSpec((tm, MLA_ROPE), row),
        pl.BlockSpec((tm, S5_WIDTH), lambda b, t: (t, b)),
        pl.BlockSpec((tm, SB_WIDTH), row), pl.BlockSpec((tm, SB_WIDTH), row), pl.BlockSpec((tm, SB_WIDTH), row),
        pl.BlockSpec((tm, SB_WIDTH), row), pl.BlockSpec((tm, SB_WIDTH), row))
    return pl.pallas_call(
        _proj_body, grid=(B, nT), in_specs=in_specs, out_specs=out_specs, out_shape=out_shape,
        compiler_params=_cparams(("parallel", "parallel")),
    )(x, sc, sh, *weights, *tabs)


def _mla_body(q_ref, k_ref, v_ref, o_ref, m_ref, l_ref, acc_ref, *, tq, scale):
    qi = pl.program_id(2)
    m_ref[...] = jnp.full(m_ref.shape, -jnp.inf, F32)
    l_ref[...] = jnp.zeros(l_ref.shape, F32)
    acc_ref[...] = jnp.zeros(acc_ref.shape, F32)
    q = q_ref[...]

    def block(kb, masked):
        r0 = pl.multiple_of(kb * tq, tq)
        k = k_ref[pl.ds(r0, tq), :]
        v = v_ref[pl.ds(r0, tq), :]
        for hh in range(2):
            sl = slice(hh * HEAD_PAD, (hh + 1) * HEAD_PAD)
            s = _dot_nt(q[:, sl], k[:, sl]) * scale
            if masked:
                rowi = lax.broadcasted_iota(jnp.int32, s.shape, 0)
                coli = lax.broadcasted_iota(jnp.int32, s.shape, 1)
                s = jnp.where(coli <= rowi, s, -jnp.inf)
            m_prev = m_ref[hh]
            m_new = jnp.maximum(m_prev, jnp.max(s, axis=-1, keepdims=True))
            alpha = jnp.exp(m_prev - m_new)
            p = jnp.exp(s - m_new[:, :1])
            l_ref[hh] = alpha * l_ref[hh] + jnp.sum(p, axis=-1, keepdims=True)
            acc_ref[hh] = alpha * acc_ref[hh] + _dot(p.astype(BF16), v)
            m_ref[hh] = m_new

    def loop_body(kb, carry):
        block(kb, False)
        return carry

    lax.fori_loop(0, qi, loop_body, 0)
    block(qi, True)
    lane = lax.broadcasted_iota(jnp.int32, (tq, LANE), 1)
    o_ref[...] = jnp.where(lane < MLA_V, acc_ref[0] / l_ref[0], acc_ref[1] / l_ref[1]).astype(o_ref.dtype)


def _mla_prompt(q, k, v, *, B, T, tq):
    nq = T // tq
    npair = MLA_HEADS // 2
    scale = (MLA_NOPE + MLA_ROPE) ** -0.5
    return pl.pallas_call(
        functools.partial(_mla_body, tq=tq, scale=scale),
        grid=(B, npair, nq),
        in_specs=[pl.BlockSpec((tq, 2 * HEAD_PAD), lambda b, h, i: (b * nq + i, h)),
                  pl.BlockSpec((T, 2 * HEAD_PAD), lambda b, h, i: (b, h)),
                  pl.BlockSpec((T, 2 * MLA_V), lambda b, h, i: (b, h))],
        out_specs=pl.BlockSpec((tq, 2 * MLA_V), lambda b, h, i: (b * nq + i, h)),
        out_shape=jax.ShapeDtypeStruct((B * T, MLA_HEADS * MLA_V), BF16),
        scratch_shapes=[pltpu.VMEM((2, tq, LANE), F32), pltpu.VMEM((2, tq, LANE), F32),
                        pltpu.VMEM((2, tq, LANE), F32)],
        compiler_params=_cparams(("parallel", "parallel", "parallel")),
    )(q, k, v)


def _log_sigmoid_pair(z):
    lb = jnp.minimum(z, 0.0) - jnp.log(1.0 + jnp.exp(-jnp.abs(z)))
    return lb, lb - z


def _split_bf16(x):
    hi = x.astype(BF16)
    return hi, (x - hi.astype(F32)).astype(BF16)


def _sb_body(q_ref, k_ref, v_ref, uo_ref, o_ref, acc_ref, c_ref, *, tq):
    qi = pl.program_id(1)
    acc_ref[...] = jnp.zeros(acc_ref.shape, F32)
    c_ref[...] = jnp.zeros(c_ref.shape, F32)
    lane = lax.broadcasted_iota(jnp.int32, (tq, LANE), 1)
    qh = []
    for pr in range(SB_HEADS // 2):
        q = q_ref[:, pr * LANE:(pr + 1) * LANE]
        zero = jnp.zeros_like(q)
        qh += [jnp.where(lane < SB_HEAD_DIM, q, zero), jnp.where(lane < SB_HEAD_DIM, zero, q)]
    uo = uo_ref[...]

    def block(kb, masked):
        r0 = pl.multiple_of(kb * tq, tq)
        for h in range(SB_HEADS):
            pr = h // 2
            k = k_ref[pl.ds(r0, tq), pr * LANE:(pr + 1) * LANE]
            v = v_ref[pl.ds(r0, tq), pr * LANE:(pr + 1) * LANE]
            z = _dot_nt(qh[h], k)
            lb, lk = _log_sigmoid_pair(z)
            if masked:
                rowi = lax.broadcasted_iota(jnp.int32, z.shape, 0)
                coli = lax.broadcasted_iota(jnp.int32, z.shape, 1)
                valid = coli < rowi
                lb = jnp.where(valid, lb, -jnp.inf)
                lk = jnp.where(valid, lk, 0.0)
            hi, lo = _split_bf16(lk)
            r = _dot(hi, uo) + _dot(lo, uo)
            w = jnp.exp(lb + r[:, :tq] + c_ref[h])
            acc_ref[h] = acc_ref[h] + _dot(w.astype(BF16), v)
            c_ref[h] = c_ref[h] + r[:, tq:]

    block(qi, True)

    def loop_body(j, carry):
        block(qi - 1 - j, False)
        return carry

    lax.fori_loop(0, qi, loop_body, 0)
    for pr in range(SB_HEADS // 2):
        o_ref[:, pr * LANE:(pr + 1) * LANE] = jnp.where(
            lane < SB_HEAD_DIM, acc_ref[2 * pr], acc_ref[2 * pr + 1]).astype(o_ref.dtype)


def _suffix_ones(n):
    j = np.arange(n)[:, None]
    s = np.arange(n)[None, :]
    return jnp.asarray(np.concatenate([(j > s), np.ones((n, n), bool)], axis=1), BF16)


def _sb_prompt(q, k, v, *, B, T, tq):
    nq = T // tq
    return pl.pallas_call(
        functools.partial(_sb_body, tq=tq),
        grid=(B, nq),
        in_specs=[pl.BlockSpec((tq, SB_WIDTH), lambda b, i: (b * nq + i, 0)),
                  pl.BlockSpec((T, SB_WIDTH), lambda b, i: (b, 0)),
                  pl.BlockSpec((T, SB_WIDTH), lambda b, i: (b, 0)),
                  pl.BlockSpec((tq, 2 * tq), lambda b, i: (0, 0))],
        out_specs=pl.BlockSpec((tq, SB_WIDTH), lambda b, i: (b * nq + i, 0)),
        out_shape=jax.ShapeDtypeStruct((B * T, SB_WIDTH), BF16),
        scratch_shapes=[pltpu.VMEM((SB_HEADS, tq, LANE), F32), pltpu.VMEM((SB_HEADS, tq, tq), F32)],
        compiler_params=_cparams(("parallel", "parallel")),
    )(q, k, v, _suffix_ones(tq))


def _s5_body(u_ref, h0r_ref, h0i_ref, ar_ref, ai_ref, bbr_ref, bbi_ref, cr_ref, ci_ref, d_ref, wg_ref, bg_ref,
             o_ref, sr_ref, si_ref, hr_s, hi_s, hsr, hsi, *, B, Tc):
    @pl.when(pl.program_id(0) == 0)
    def _():
        hr_s[...] = h0r_ref[...]
        hi_s[...] = h0i_ref[...]

    u = u_ref[...]
    ub = u.astype(BF16)
    for j in range(S5_NBLK):
        uj = ub[:, j * LANE:(j + 1) * LANE]
        hsr[:, j * S5_BLK_LANES:(j + 1) * S5_BLK_LANES] = _dot(uj, bbr_ref[j])
        hsi[:, j * S5_BLK_LANES:(j + 1) * S5_BLK_LANES] = _dot(uj, bbi_ref[j])
    ar, ai = ar_ref[...], ai_ref[...]

    def step(t, carry):
        hr, hi = carry
        r0 = pl.multiple_of(t * B, B)
        nr = ar * hr - ai * hi + hsr[pl.ds(r0, B), :]
        ni = ar * hi + ai * hr + hsi[pl.ds(r0, B), :]
        hsr[pl.ds(r0, B), :] = nr
        hsi[pl.ds(r0, B), :] = ni
        return nr, ni

    hr, hi = lax.fori_loop(0, Tc, step, (hr_s[...], hi_s[...]))
    hr_s[...] = hr
    hi_s[...] = hi
    sr_ref[...] = hr
    si_ref[...] = hi

    ys = []
    for j in range(S5_NBLK):
        sl = slice(j * S5_BLK_LANES, (j + 1) * S5_BLK_LANES)
        ys.append(_dot(hsr[:, sl].astype(BF16), cr_ref[j]) - _dot(hsi[:, sl].astype(BF16), ci_ref[j]))
    y = jnp.concatenate(ys, axis=1) + d_ref[...] * u
    y = jax.nn.gelu(y)
    gate = _dot(y.astype(BF16), wg_ref[...]) + bg_ref[...]
    o_ref[...] = (y * jax.nn.sigmoid(gate)).astype(o_ref.dtype)


def _s5(u, h0r, h0i, lw, *, B, T, Tc):
    rows = Tc * B
    full2 = lambda i: (0, 0)
    full3 = lambda i: (0, 0, 0)
    st = jax.ShapeDtypeStruct((B, S5_LANES), F32)
    return pl.pallas_call(
        functools.partial(_s5_body, B=B, Tc=Tc),
        grid=(T // Tc,),
        in_specs=[pl.BlockSpec((rows, S5_WIDTH), lambda i: (i, 0)),
                  pl.BlockSpec((B, S5_LANES), full2), pl.BlockSpec((B, S5_LANES), full2),
                  pl.BlockSpec((1, S5_LANES), full2), pl.BlockSpec((1, S5_LANES), full2),
                  pl.BlockSpec((S5_NBLK, LANE, S5_BLK_LANES), full3),
                  pl.BlockSpec((S5_NBLK, LANE, S5_BLK_LANES), full3),
                  pl.BlockSpec((S5_NBLK, S5_BLK_LANES, LANE), full3),
                  pl.BlockSpec((S5_NBLK, S5_BLK_LANES, LANE), full3),
                  pl.BlockSpec((1, S5_WIDTH), full2), pl.BlockSpec((S5_WIDTH, S5_WIDTH), full2),
                  pl.BlockSpec((1, S5_WIDTH), full2)],
        out_specs=(pl.BlockSpec((rows, S5_WIDTH), lambda i: (i, 0)),
                   pl.BlockSpec((B, S5_LANES), full2), pl.BlockSpec((B, S5_LANES), full2)),
        out_shape=(jax.ShapeDtypeStruct((T * B, S5_WIDTH), BF16), st, st),
        scratch_shapes=[pltpu.VMEM((B, S5_LANES), F32), pltpu.VMEM((B, S5_LANES), F32),
                        pltpu.VMEM((rows, S5_LANES), F32), pltpu.VMEM((rows, S5_LANES), F32)],
        compiler_params=_cparams(("arbitrary",)),
    )(u, h0r, h0i, lw['a_re'], lw['a_im'], lw['bb_re'], lw['bb_im'], lw['cc_re'], lw['cc_im'],
      lw['s5_d'], lw['w_glu'], lw['b_glu'])


def _out_body(x_ref, oa_ref, ob_ref, oc_ref, gt_ref, sc_ref, sh_ref, wo_ref, gffn_ref, wrh_ref, wrl_ref, br_ref,
              x1_ref, h2_ref, ti_ref, tg_ref):
    wa = MLA_HEADS * MLA_V
    o = (_dot(oa_ref[...], wo_ref[0:wa, :]) + _dot(ob_ref[...], wo_ref[wa:wa + S5_WIDTH, :])
         + _dot(oc_ref[...], wo_ref[wa + S5_WIDTH:, :]))
    x1 = x_ref[...] + gt_ref[0] * o
    x1_ref[...] = x1
    h2 = _rms(x1, gffn_ref[...]) * (1.0 + sc_ref[0]) + sh_ref[0]
    hb = h2.astype(BF16).astype(F32)
    h2_ref[...] = (pltpu.bitcast(hb[:, :D_MODEL // 2], jnp.uint32)
                   | (pltpu.bitcast(hb[:, D_MODEL // 2:], jnp.uint32) >> 16))
    hi, lo = _split_bf16(h2)
    logits = _dot(hi, wrh_ref[...]) + _dot(hi, wrl_ref[...]) + _dot(lo, wrh_ref[...]) + br_ref[...]
    lane = lax.broadcasted_iota(jnp.int32, logits.shape, 1)
    lanef = lane.astype(F32)
    vals, idxs = [], []
    for _ in range(TOP_K):
        m = jnp.max(logits, axis=-1, keepdims=True)
        idx = jnp.min(jnp.where(logits == m, lanef, float(ROUTER_PAD)), axis=-1, keepdims=True).astype(jnp.int32)
        vals.append(m)
        idxs.append(idx)
        logits = jnp.where(lane == idx, -jnp.inf, logits)
    es = [jnp.exp(v - vals[0]) for v in vals]
    den = es[0] + es[1] + es[2] + es[3]
    ti = jnp.zeros(lane.shape, jnp.int32)
    tg = jnp.zeros(lane.shape, F32)
    for kk in range(TOP_K):
        ti = jnp.where(lane == kk, idxs[kk], ti)
        tg = jnp.where(lane == kk, es[kk] / den, tg)
    ti_ref[...] = ti
    tg_ref[...] = tg


def _outproj(x, oa, ob, oc, gt, sc, sh, lw, *, B, T, tm):
    N = B * T
    nT = T // tm
    mt = gt.shape[1]
    row = lambda b, t: (b * nT + t, 0)
    full = lambda b, t: (0, 0)
    if mt == 1:
        mod_spec = pl.BlockSpec((1, 1, D_MODEL), lambda b, t: (b, 0, 0))
    else:
        mod_spec = pl.BlockSpec((1, tm, D_MODEL), lambda b, t: (b, t, 0))
    weights = [lw['w_out'], lw['g_ffn'], lw['wr_hi'], lw['wr_lo'], lw['b_router']]
    return pl.pallas_call(
        _out_body, grid=(B, nT),
        in_specs=[pl.BlockSpec((tm, D_MODEL), row), pl.BlockSpec((tm, MLA_HEADS * MLA_V), row),
                  pl.BlockSpec((tm, S5_WIDTH), lambda b, t: (t, b)), pl.BlockSpec((tm, SB_WIDTH), row),
                  mod_spec, mod_spec, mod_spec] + [pl.BlockSpec(w.shape, full) for w in weights],
        out_specs=(pl.BlockSpec((tm, D_MODEL), row), pl.BlockSpec((tm, D_MODEL // 2), row),
                   pl.BlockSpec((tm, ROUTER_PAD), row), pl.BlockSpec((tm, ROUTER_PAD), row)),
        out_shape=(jax.ShapeDtypeStruct((N, D_MODEL), F32), jax.ShapeDtypeStruct((N, D_MODEL // 2), jnp.uint32),
                   jax.ShapeDtypeStruct((N, ROUTER_PAD), jnp.int32), jax.ShapeDtypeStruct((N, ROUTER_PAD), F32)),
        compiler_params=_cparams(("parallel", "parallel")),
    )(x, oa, ob, oc, gt, sc, sh, *weights)


def _moe_body(be_ref, nu_ref, x_ref, wg_ref, wu_ref, wd_ref, bg_ref, bu_ref, bd_ref, o_ref, wd16_ref):
    i = pl.program_id(0)
    used = i < nu_ref[0]

    @pl.when(used & ((i == 0) | (be_ref[i] != be_ref[jnp.maximum(i - 1, 0)])))
    def _():
        wd16_ref[...] = wd_ref[0, 0].astype(BF16)

    @pl.when(used)
    def _():
        w = x_ref[...]
        x = jnp.concatenate([pltpu.bitcast(w & jnp.uint32(0xFFFF0000), F32), pltpu.bitcast(w << 16, F32)],
                            axis=1).astype(BF16)
        g = jnp.minimum(_dot(x, wg_ref[0]) + bg_ref[0], SWIGLU_LIMIT)
        up = jnp.clip(_dot(x, wu_ref[0]) + bu_ref[0], -SWIGLU_LIMIT, SWIGLU_LIMIT)
        act = (up + 1.0) * (g * jax.nn.sigmoid(SWIGLU_ALPHA * g))
        o_ref[...] = _dot(act.astype(BF16), wd16_ref[...]) + bd_ref[0]

    @pl.when(i >= nu_ref[0])
    def _():
        o_ref[...] = jnp.zeros(o_ref.shape, F32)


def _moe(block_expert, n_used, xs, lw, *, bm, layer):
    n_slots = xs.shape[0]
    n_blocks = n_slots // bm
    wsel = lambda i, be, nu: (be[i], 0, 0)
    row = lambda i, be, nu: (i, 0)
    return pl.pallas_call(
        _moe_body,
        grid_spec=pltpu.PrefetchScalarGridSpec(
            num_scalar_prefetch=2, grid=(n_blocks,),
            in_specs=[pl.BlockSpec((bm, D_MODEL // 2), row),
                      pl.BlockSpec((1, D_MODEL, D_FF), wsel), pl.BlockSpec((1, D_MODEL, D_FF), wsel),
                      pl.BlockSpec((1, 1, D_FF, D_MODEL), lambda i, be, nu: (layer, be[i], 0, 0)),
                      pl.BlockSpec((1, 1, D_FF), wsel), pl.BlockSpec((1, 1, D_FF), wsel),
                      pl.BlockSpec((1, 1, D_MODEL), wsel)],
            out_specs=pl.BlockSpec((bm, D_MODEL), row),
            scratch_shapes=[pltpu.VMEM((D_FF, D_MODEL), BF16)]),
        out_shape=jax.ShapeDtypeStruct((n_slots, D_MODEL), F32),
        compiler_params=_cparams(("arbitrary",)),
    )(block_expert, n_used, xs, lw['w_gate'], lw['w_up'], lw['w_down'], lw['b_gate'], lw['b_up'], lw['b_down'])


def _route(topi, bm):
    n = topi.shape[0]
    tk = n * TOP_K
    n_blocks = -(-tk // bm) + N_EXPERTS
    n_fill = n_blocks * bm - tk
    flat_e = topi[:, :TOP_K].reshape(tk)
    counts = jnp.sum((flat_e[:, None] == jnp.arange(N_EXPERTS, dtype=jnp.int32)[None, :]).astype(jnp.int32), axis=0)
    blocks_per_e = (counts + bm - 1) // bm
    block_end = jnp.cumsum(blocks_per_e)
    pads = blocks_per_e * bm - counts
    f = jnp.arange(n_fill, dtype=jnp.int32)
    f_e, f_r = f // bm, f % bm
    f_pads = jnp.concatenate([jnp.repeat(pads, bm), jnp.zeros((n_fill - N_EXPERTS * bm,), jnp.int32)])
    keys = jnp.concatenate([2 * flat_e, jnp.where(f_r < f_pads, 2 * f_e + 1, 2 * N_EXPERTS)])
    order = jnp.argsort(keys).astype(jnp.int32)
    slot_tok = jnp.where(order < tk, order // TOP_K, 0)
    dest = jnp.argsort(order).astype(jnp.int32)[:tk]
    blk = jnp.arange(n_blocks, dtype=jnp.int32)
    block_expert = jnp.minimum(jnp.sum((block_end[None, :] <= blk[:, None]).astype(jnp.int32), axis=1),
                               N_EXPERTS - 1)
    return slot_tok, dest, block_expert, block_end[-1:].astype(jnp.int32)


SC_WINDOW_BYTES = 128 * 1024


def _gather_rows(x, idx):
    n_idx = idx.shape[0]
    d = x.shape[1]
    window = SC_WINDOW_BYTES // (d * x.dtype.itemsize)
    info = plsc.get_sparse_core_info()
    nc, ns = info.num_cores, info.num_subcores
    per_w = n_idx // (nc * ns)
    if n_idx % (nc * ns * window) != 0:
        return jnp.take(x, idx, axis=0)
    mesh = plsc.VectorSubcoreMesh(core_axis_name="c", subcore_axis_name="s")

    @functools.partial(
        pl.kernel, out_type=jax.ShapeDtypeStruct((n_idx, d), x.dtype), mesh=mesh,
        scratch_types=[pltpu.VMEM((window,), jnp.int32), pltpu.VMEM((window, d), x.dtype),
                       pltpu.SemaphoreType.DMA])
    def gather_kernel(x_hbm, i_hbm, o_hbm, idx_v, rows_v, sem):
        base = (lax.axis_index("s") * nc + lax.axis_index("c")) * per_w

        @pl.loop(0, per_w // window)
        def _(j):
            off = base + j * window
            pltpu.sync_copy(i_hbm.at[pl.ds(off, window)], idx_v)
            pltpu.async_copy(x_hbm.at[idx_v], rows_v, sem).wait()
            pltpu.sync_copy(rows_v, o_hbm.at[pl.ds(off, window)])

    return gather_kernel(x, idx)


def _combine_body(x_ref, y_ref, g_ref, gt_ref, o_ref):
    g = g_ref[...]
    y = y_ref[0] * g[:, 0:1]
    for kk in range(1, TOP_K):
        y = y + y_ref[kk] * g[:, kk:kk + 1]
    o_ref[...] = x_ref[...] + gt_ref[0] * y


def _combine(x1, yk, topg, gt, *, B, T, tm):
    nT = T // tm
    mt = gt.shape[1]
    row = lambda b, t: (b * nT + t, 0)
    if mt == 1:
        mod_spec = pl.BlockSpec((1, 1, D_MODEL), lambda b, t: (b, 0, 0))
    else:
        mod_spec = pl.BlockSpec((1, tm, D_MODEL), lambda b, t: (b, t, 0))
    return pl.pallas_call(
        _combine_body, grid=(B, nT),
        in_specs=[pl.BlockSpec((tm, D_MODEL), row),
                  pl.BlockSpec((TOP_K, tm, D_MODEL), lambda b, t: (0, b * nT + t, 0)),
                  pl.BlockSpec((tm, ROUTER_PAD), row), mod_spec],
        out_specs=pl.BlockSpec((tm, D_MODEL), row),
        out_shape=jax.ShapeDtypeStruct(x1.shape, F32),
        compiler_params=_cparams(("parallel", "parallel")),
    )(x1, yk, topg, gt)


def _page_copies(pt_ref, b, c, slot, srcs, bufs, sems, keys_on_lanes, *, layer, G):
    out = []
    for g in range(G):
        page = pt_ref[b, c * G + g]
        for src, buf, sem, on_lanes in zip(srcs, bufs, sems, keys_on_lanes):
            keys = pl.ds(g * PAGE_SIZE, PAGE_SIZE)
            dst = buf.at[slot, :, keys] if on_lanes else buf.at[slot, keys]
            out.append(pltpu.make_async_copy(src.at[layer, page], dst, sem.at[slot]))
    return out


def _mla_dec_body(pt_ref, qm_ref, qpe_ref, q8_ref, k8_ref, cn_ref, wukt_ref, wuv_ref, kv_hbm, kr_hbm,
                  o_ref, kvbuf, krbuf, sem_kv, sem_kr, m_ref, l_ref, pc_ref, lhs_ref, *, layer, G, nc, nb, scale):
    b = pl.program_id(0)
    c = pl.program_id(1)
    step = b * nc + c
    slot = lax.rem(step, 2)
    copies = functools.partial(_page_copies, pt_ref, srcs=(kv_hbm, kr_hbm), bufs=(kvbuf, krbuf),
                               sems=(sem_kv, sem_kr), keys_on_lanes=(False, True), layer=layer, G=G)

    @pl.when(step == 0)
    def _():
        for cp in copies(0, 0, 0):
            cp.start()

    @pl.when(step + 1 < nb * nc)
    def _():
        nxt = step + 1
        for cp in copies(nxt // nc, lax.rem(nxt, nc), 1 - slot):
            cp.start()

    @pl.when(c == 0)
    def _():
        m_ref[...] = jnp.full(m_ref.shape, -jnp.inf, F32)
        l_ref[...] = jnp.zeros(l_ref.shape, F32)
        pc_ref[...] = jnp.zeros(pc_ref.shape, F32)
        nk = MLA_HEADS * MLA_NOPE
        lhs_ref[0:nk, :] = wukt_ref[...]
        qabs = _dot(qm_ref[0], wukt_ref[...])
        hi = qabs.astype(BF16).astype(F32)
        lhs_ref[nk:nk + 16, :] = jnp.concatenate([hi, qabs - hi], axis=0).astype(BF16)

    for cp in copies(b, c, slot):
        cp.wait()

    nk = MLA_HEADS * MLA_NOPE
    cb = kvbuf[slot].astype(BF16)
    res = _dot_nt(lhs_ref[...], cb)
    sn = res[nk:nk + 8, :] + res[nk + 8:nk + 16, :]
    ss = []
    for h in range(MLA_HEADS):
        kh = res[h * MLA_NOPE:(h + 1) * MLA_NOPE, :]
        ss.append(jnp.sum(kh * kh, axis=0, keepdims=True))
    ss = jnp.concatenate(ss + [jnp.zeros((8 - MLA_HEADS, res.shape[1]), F32)], axis=0)
    s = sn * lax.rsqrt(ss * (1.0 / MLA_NOPE) + EPS)
    s = (s + _dot(qpe_ref[0], krbuf[slot].astype(BF16))) * scale
    m_prev = m_ref[...]
    m_new = jnp.maximum(m_prev, jnp.max(s, axis=-1, keepdims=True))
    alpha = jnp.exp(m_prev - m_new)
    p = jnp.exp(s - m_new[:, :1])
    l_ref[...] = alpha * l_ref[...] + jnp.sum(p, axis=-1, keepdims=True)
    pc_ref[...] = alpha * pc_ref[...] + _dot(p.astype(BF16), cb)
    m_ref[...] = m_new

    @pl.when(c == nc - 1)
    def _():
        s_new = jnp.sum(q8_ref[0].astype(F32) * k8_ref[0].astype(F32), axis=-1, keepdims=True) * scale
        m_f = jnp.maximum(m_ref[...], s_new)
        a_old = jnp.exp(m_ref[...] - m_f)
        p_new = jnp.exp(s_new - m_f)
        l_f = a_old * l_ref[...] + p_new
        ctx = (a_old * pc_ref[...] + p_new * cn_ref[0]) / l_f
        full = _dot(ctx.astype(BF16), wuv_ref[...])
        rowi = lax.broadcasted_iota(jnp.int32, full.shape, 0)
        coli = lax.broadcasted_iota(jnp.int32, full.shape, 1)
        own = (coli >= rowi * MLA_V) & (coli < (rowi + 1) * MLA_V)
        o_ref[0] = jnp.sum(jnp.where(own, full, 0.0), axis=0, keepdims=True).astype(o_ref.dtype)


def _mla_decode(page_table, qm, qpe, q8, k8, cn, lw, cache_kv, cache_krt, *, layer):
    nb, n_pages = page_table.shape
    G = min(32, n_pages)
    nc = n_pages // G
    rows = G * PAGE_SIZE
    scale = (MLA_NOPE + MLA_ROPE) ** -0.5
    per_b = lambda b, c, pt: (b, 0, 0)
    full = lambda b, c, pt: (0, 0)
    return pl.pallas_call(
        functools.partial(_mla_dec_body, layer=layer, G=G, nc=nc, nb=nb, scale=scale),
        grid_spec=pltpu.PrefetchScalarGridSpec(
            num_scalar_prefetch=1, grid=(nb, nc),
            in_specs=[pl.BlockSpec((1, 8, MLA_HEADS * MLA_NOPE), per_b), pl.BlockSpec((1, 8, MLA_ROPE), per_b),
                      pl.BlockSpec((1, 8, HEAD_PAD), per_b), pl.BlockSpec((1, 8, HEAD_PAD), per_b),
                      pl.BlockSpec((1, 1, MLA_KV_LORA), per_b),
                      pl.BlockSpec(lw['wukt'].shape, full), pl.BlockSpec(lw['wuv'].shape, full),
                      pl.BlockSpec(memory_space=pl.ANY), pl.BlockSpec(memory_space=pl.ANY)],
            out_specs=pl.BlockSpec((1, 1, MLA_HEADS * MLA_V), per_b),
            scratch_shapes=[pltpu.VMEM((2, rows, MLA_KV_LORA), F32), pltpu.VMEM((2, MLA_ROPE, rows), F32),
                            pltpu.SemaphoreType.DMA((2,)), pltpu.SemaphoreType.DMA((2,)),
                            pltpu.VMEM((8, LANE), F32), pltpu.VMEM((8, LANE), F32),
                            pltpu.VMEM((8, MLA_KV_LORA), F32),
                            pltpu.VMEM((MLA_HEADS * MLA_NOPE + 16, MLA_KV_LORA), BF16)]),
        out_shape=jax.ShapeDtypeStruct((nb, 1, MLA_HEADS * MLA_V), BF16),
        compiler_params=_cparams(("arbitrary", "arbitrary")),
    )(page_table, qm, qpe, q8, k8, cn, lw['wukt'], lw['wuv'], cache_kv, cache_krt)


def _sb_dec_body(pt_ref, q_ref, uo_ref, sfx_ref, k_hbm, v_hbm, o_ref, kbuf, vbuf, sem_k, sem_v, acc_ref, c_ref,
                 *, layer, G, nc, nb):
    b = pl.program_id(0)
    ci = pl.program_id(1)
    step = b * nc + ci
    slot = lax.rem(step, 2)
    copies = functools.partial(_page_copies, pt_ref, srcs=(k_hbm, v_hbm), bufs=(kbuf, vbuf),
                               sems=(sem_k, sem_v), keys_on_lanes=(True, True), layer=layer, G=G)

    @pl.when(step == 0)
    def _():
        for cp in copies(0, nc - 1, 0):
            cp.start()

    @pl.when(step + 1 < nb * nc)
    def _():
        nxt = step + 1
        for cp in copies(nxt // nc, nc - 1 - lax.rem(nxt, nc), 1 - slot):
            cp.start()

    @pl.when(ci == 0)
    def _():
        acc_ref[...] = jnp.zeros(acc_ref.shape, F32)
        c_ref[...] = jnp.zeros(c_ref.shape, F32)

    for cp in copies(b, nc - 1 - ci, slot):
        cp.wait()

    q = q_ref[0]
    kt = kbuf[slot].astype(BF16)
    vt = vbuf[slot].astype(BF16)
    page = lambda a, g: a[:, g * PAGE_SIZE:(g + 1) * PAGE_SIZE]
    z = jnp.concatenate([_dot(q, page(kt, g)) for g in range(G)], axis=0)
    lb, lk = _log_sigmoid_pair(z)
    hi, lo = _split_bf16(lk)
    r = _dot(hi, uo_ref[...]) + _dot(lo, uo_ref[...])
    th, tl = _split_bf16(r[:, PAGE_SIZE:])
    later = _dot(sfx_ref[...], th) + _dot(sfx_ref[...], tl)
    c = c_ref[...]
    w = jnp.exp(lb + r[:, :PAGE_SIZE] + later + jnp.concatenate([c] * G, axis=0))
    acc = acc_ref[...]
    for g in range(G):
        acc = acc + _dot_nt(w[g * 8:(g + 1) * 8, :].astype(BF16), page(vt, g))
    acc_ref[...] = acc
    c_ref[...] = c + later[0:8, :] + r[0:8, PAGE_SIZE:]

    @pl.when(ci == nc - 1)
    def _():
        acc = acc_ref[...]
        rowi = lax.broadcasted_iota(jnp.int32, acc.shape, 0)
        coli = lax.broadcasted_iota(jnp.int32, acc.shape, 1)
        own = (coli >= rowi * SB_HEAD_DIM) & (coli < (rowi + 1) * SB_HEAD_DIM)
        o_ref[0] = jnp.sum(jnp.where(own, acc, 0.0), axis=0, keepdims=True).astype(o_ref.dtype)


def _sb_decode(page_table, qs, cache_kt, cache_vt, *, layer):
    nb, n_pages = page_table.shape
    G = min(16, n_pages)
    nc = n_pages // G
    rows = G * PAGE_SIZE
    r = np.arange(8 * G)
    sfx = jnp.asarray((r[None, :] % 8 == r[:, None] % 8) & (r[None, :] // 8 > r[:, None] // 8), BF16)
    per_b = lambda b, c, pt: (b, 0, 0)
    full = lambda b, c, pt: (0, 0)
    return pl.pallas_call(
        functools.partial(_sb_dec_body, layer=layer, G=G, nc=nc, nb=nb),
        grid_spec=pltpu.PrefetchScalarGridSpec(
            num_scalar_prefetch=1, grid=(nb, nc),
            in_specs=[pl.BlockSpec((1, 8, SB_WIDTH), per_b), pl.BlockSpec((PAGE_SIZE, 2 * PAGE_SIZE), full),
                      pl.BlockSpec((8 * G, 8 * G), full),
                      pl.BlockSpec(memory_space=pl.ANY), pl.BlockSpec(memory_space=pl.ANY)],
            out_specs=pl.BlockSpec((1, 1, SB_WIDTH), per_b),
            scratch_shapes=[pltpu.VMEM((2, SB_WIDTH, rows), F32), pltpu.VMEM((2, SB_WIDTH, rows), F32),
                            pltpu.SemaphoreType.DMA((2,)), pltpu.SemaphoreType.DMA((2,)),
                            pltpu.VMEM((8, SB_WIDTH), F32), pltpu.VMEM((8, LANE), F32)]),
        out_shape=jax.ShapeDtypeStruct((nb, 1, SB_WIDTH), BF16),
        compiler_params=_cparams(("arbitrary", "arbitrary")),
    )(page_table, qs, _suffix_ones(PAGE_SIZE), sfx, cache_kt, cache_vt)


def _rope_tables(pos):
    half = MLA_ROPE // 2
    inv = ROPE_THETA ** (-jnp.arange(half, dtype=F32) / half)
    ang = pos.astype(F32)[:, None] * inv
    cos, sin = jnp.cos(ang), jnp.sin(ang)
    n = pos.shape[0]
    z16 = jnp.zeros((n, half), F32)
    z32 = jnp.zeros((n, HEAD_PAD - MLA_NOPE - MLA_ROPE), F32)
    z64 = jnp.zeros((n, MLA_NOPE), F32)
    cos_t = jnp.concatenate([jnp.ones((n, MLA_NOPE), F32), cos, cos, z32], axis=1)
    sin_a = jnp.concatenate([z64, -sin, z16, z32], axis=1)
    sin_b = jnp.concatenate([z64, z16, sin, z32], axis=1)
    return cos_t, sin_a, sin_b


def _pad_heads(w, width):
    K = w.shape[0]
    w = w.reshape(K, MLA_HEADS, width)
    w = jnp.pad(w, ((0, 0), (0, 0), (0, HEAD_PAD - width)))
    return w.reshape(K, MLA_HEADS * HEAD_PAD)


def _block_diag(w):
    a, b = w.shape[1], w.shape[2]
    w = w.reshape(S5_NBLK, S5_BLK_GROUPS, a, b)
    eye = jnp.eye(S5_BLK_GROUPS, dtype=w.dtype)
    out = jnp.einsum('jgab,gh->jgahb', w, eye)
    return out.reshape(S5_NBLK, S5_BLK_GROUPS * a, S5_BLK_GROUPS * b)


def _layer_weights(l, p, disc):
    a_re, a_im, f_re, f_im = [d[l] for d in disc]
    lw = {}
    w_in = p['w_in'][l]
    kpe_cols = jnp.pad(w_in[:, 384:416], ((0, 0), (MLA_NOPE, HEAD_PAD - MLA_NOPE - MLA_ROPE)))
    lw['w1'] = jnp.concatenate([w_in[:, 0:384], kpe_cols, w_in[:, 416:]], axis=1).astype(BF16)
    row = lambda v: v.reshape(1, -1).astype(F32)
    lw['g_mix'] = row(p['g_mix'][l])
    lw['g_q_lat'] = row(p['g_q_lat'][l])
    lw['g_kv_lat'] = row(p['g_kv_lat'][l])
    lw['wuq'] = _pad_heads(p['w_uq'][l], MLA_NOPE + MLA_ROPE).astype(BF16)
    lw['wuk'] = _pad_heads(p['w_uk'][l], MLA_NOPE).astype(BF16)
    lw['wukt'] = p['w_uk'][l].T.astype(BF16)
    lw['wuv'] = p['w_uv'][l].astype(BF16)
    zpad = jnp.zeros((HEAD_PAD - MLA_NOPE - MLA_ROPE,), F32)
    lw['gq'] = row(jnp.concatenate([p['g_qn_nope'][l], p['g_qn_rope'][l], zpad]))
    lw['gk'] = row(jnp.concatenate([p['g_kn_nope'][l], jnp.zeros((HEAD_PAD - MLA_NOPE,), F32)]))
    lw['gkr'] = row(jnp.concatenate([jnp.zeros((MLA_NOPE,), F32), p['g_kn_rope'][l], zpad]))
    br, bi = p['s5_b_re'][l], p['s5_b_im'][l]
    bb_re = f_re[..., None] * br - f_im[..., None] * bi
    bb_im = f_re[..., None] * bi + f_im[..., None] * br
    lw['bb_re'] = _block_diag(jnp.swapaxes(bb_re, 1, 2)).astype(BF16)
    lw['bb_im'] = _block_diag(jnp.swapaxes(bb_im, 1, 2)).astype(BF16)
    lw['cc_re'] = _block_diag(jnp.swapaxes(p['s5_c_re'][l], 1, 2)).astype(BF16)
    lw['cc_im'] = _block_diag(jnp.swapaxes(p['s5_c_im'][l], 1, 2)).astype(BF16)
    lw['a_re'] = a_re.reshape(1, S5_LANES)
    lw['a_im'] = a_im.reshape(1, S5_LANES)
    lw['s5_d'] = row(p['s5_d'][l])
    lw['w_glu'] = p['w_glu'][l].astype(BF16)
    lw['b_glu'] = row(p['b_glu'][l])
    lw['w_out'] = p['w_out'][l].astype(BF16)
    lw['g_ffn'] = row(p['g_ffn'][l])
    wr = jnp.pad(p['w_router'][l], ((0, 0), (0, ROUTER_PAD - N_EXPERTS)))
    lw['wr_hi'] = wr.astype(BF16)
    lw['wr_lo'] = (wr - lw['wr_hi'].astype(F32)).astype(BF16)
    lw['b_router'] = row(jnp.concatenate([p['b_router'][l], jnp.full((ROUTER_PAD - N_EXPERTS,), NEG_BIG, F32)]))
    w_gu = p['w_gu'][l].reshape(N_EXPERTS, D_MODEL, D_FF, 2)
    lw['w_gate'] = w_gu[..., 0].astype(BF16)
    lw['w_up'] = w_gu[..., 1].astype(BF16)
    lw['w_down'] = p['w_down']
    b_gu = p['b_gu'][l].reshape(N_EXPERTS, 1, D_FF, 2)
    lw['b_gate'] = b_gu[..., 0]
    lw['b_up'] = b_gu[..., 1]
    lw['b_down'] = p['b_down'][l].reshape(N_EXPERTS, 1, D_MODEL)
    return lw


def _split_mod(mod):
    return [mod[..., i * D_MODEL:(i + 1) * D_MODEL] for i in range(N_MOD)]


def _ffn(x1, h2, topi, topg, gt_f, lw, *, B, T, tm, bm, layer):
    n = B * T
    slot_tok, dest, block_expert, n_used = _route(topi, bm)
    xs = _gather_rows(h2, slot_tok)
    ys = _moe(block_expert, n_used, xs, lw, bm=bm, layer=layer)
    yk = _gather_rows(ys, dest.reshape(n, TOP_K).T.reshape(n * TOP_K)).reshape(TOP_K, n, D_MODEL)
    return _combine(x1, yk, topg, gt_f, B=B, T=T, tm=512 if T % 512 == 0 else tm)


def _prompt_layer(x, mods, tabs, lw, *, B, T, l):
    sh_a, sc_a, gt_a, sh_f, sc_f, gt_f = mods
    tm = min(256, T)
    q, k, v, ckv, kpe, u, sq, sk16, sv16, sk, sv = _proj(x, sc_a, sh_a, tabs, lw, B=B, T=T, tm=tm, qk_dtype=BF16)
    o_a = _mla_prompt(q, k, v, B=B, T=T, tq=512 if T % 512 == 0 else min(256, T))
    o_c = _sb_prompt(sq, sk16, sv16, B=B, T=T, tq=min(256, T))
    zeros = jnp.zeros((B, S5_LANES), F32)
    Tc = min(64, T)
    o_b, s_re, s_im = _s5(u.reshape(T * B, S5_WIDTH), zeros, zeros, lw, B=B, T=T, Tc=Tc)
    x1, h2, topi, topg = _outproj(x, o_a, o_b.reshape(T, B * S5_WIDTH), o_c, gt_a, sc_f, sh_f, lw, B=B, T=T, tm=tm)
    x2 = _ffn(x1, h2, topi, topg, gt_f, lw, B=B, T=T, tm=tm, bm=min(256, max(16, (B * T * TOP_K) // N_EXPERTS)),
              layer=l)
    state = (ckv.reshape(B, T, MLA_KV_LORA), kpe.reshape(B, T, MLA_ROPE),
             sk.reshape(B, T, SB_HEADS, SB_HEAD_DIM), sv.reshape(B, T, SB_HEADS, SB_HEAD_DIM),
             s_re.reshape(B, S5_GROUPS, S5_STATE), s_im.reshape(B, S5_GROUPS, S5_STATE))
    return x2, state


def _sample_layer(x, mods, tabs, lw, caches, h0, page_table, p, *, l, DB):
    sh_a, sc_a, gt_a, sh_f, sc_f, gt_f = mods
    cache_kv, cache_kr, cache_sk, cache_sv = caches
    q, k, v, ckv, kpe, u, sq, sk16, sv16, sk, sv = _proj(x, sc_a, sh_a, tabs, lw, B=1, T=DB, tm=DB, qk_dtype=F32)
    q8 = jnp.pad(q.reshape(DB, MLA_HEADS, HEAD_PAD), ((0, 0), (0, 8 - MLA_HEADS), (0, 0)))
    k8 = jnp.pad(k.reshape(DB, MLA_HEADS, HEAD_PAD), ((0, 0), (0, 8 - MLA_HEADS), (0, 0)))
    qg = q8[:, :, :MLA_NOPE] * p['g_kn_nope'][l][None, None, :]
    qm = jnp.einsum('bhd,hg->bhgd', qg, jnp.eye(8, MLA_HEADS, dtype=F32)).reshape(DB, 8, MLA_HEADS * MLA_NOPE)
    qpe = q8[:, :, MLA_NOPE:MLA_NOPE + MLA_ROPE].astype(BF16)
    o_a = _mla_decode(page_table, qm.astype(BF16), qpe, q8.astype(BF16), k8.astype(BF16),
                      ckv.reshape(DB, 1, MLA_KV_LORA),
                      lw, cache_kv, cache_kr, layer=l).reshape(DB, MLA_HEADS * MLA_V)
    sq4 = sq.reshape(DB, SB_HEADS, SB_HEAD_DIM)
    eye4 = jnp.eye(8, SB_HEADS, dtype=BF16)
    qs = jnp.einsum('bhd,rh,hg->brgd', sq4, eye4, jnp.eye(SB_HEADS, dtype=BF16)).reshape(DB, 8, SB_WIDTH)
    o_c = _sb_decode(page_table, qs, cache_sk, cache_sv, layer=l).reshape(DB, SB_WIDTH)
    o_b, s_re, s_im = _s5(u, h0[0], h0[1], lw, B=DB, T=1, Tc=1)
    x1, h2, topi, topg = _outproj(x, o_a, o_b, o_c, gt_a, sc_f, sh_f, lw, B=1, T=DB, tm=DB)
    x2 = _ffn(x1, h2, topi, topg, gt_f, lw, B=1, T=DB, tm=DB, bm=16, layer=l)
    state = (ckv.reshape(DB, 1, MLA_KV_LORA), kpe.reshape(DB, 1, MLA_ROPE),
             sk.reshape(DB, 1, SB_HEADS, SB_HEAD_DIM), sv.reshape(DB, 1, SB_HEADS, SB_HEAD_DIM),
             s_re.reshape(DB, S5_GROUPS, S5_STATE), s_im.reshape(DB, S5_GROUPS, S5_STATE))
    return x2, state


def kernel(x_prompt, x_sample, cache_kv_latent, cache_k_rope, cache_sb_k, cache_sb_v, state_s5_re, state_s5_im, page_table, c_prompt, c_sample, w_ada, b_ada, g_mix, w_in, g_q_lat, w_uq, g_qn_nope, g_qn_rope, g_kv_lat, g_kn_rope, w_uk, w_uv, g_kn_nope, s5_lam_re, s5_lam_im, s5_log_step, s5_b_re, s5_b_im, s5_c_re, s5_c_im, s5_d, w_glu, b_glu, w_out, g_ffn, w_router, b_router, w_gu, b_gu, w_down, b_down):
    B, T, _ = x_prompt.shape
    DB, DT, _ = x_sample.shape
    assert DT == 1, "the sample group carries one new token per row"
    L = w_ada.shape[0]
    n_pool = cache_kv_latent.shape[1]
    past_len = page_table.shape[1] * PAGE_SIZE
    p = dict(w_in=w_in, g_mix=g_mix, g_q_lat=g_q_lat, w_uq=w_uq, g_qn_nope=g_qn_nope, g_qn_rope=g_qn_rope,
             g_kv_lat=g_kv_lat, g_kn_rope=g_kn_rope, w_uk=w_uk, w_uv=w_uv, g_kn_nope=g_kn_nope,
             s5_b_re=s5_b_re, s5_b_im=s5_b_im, s5_c_re=s5_c_re, s5_c_im=s5_c_im, s5_d=s5_d, w_glu=w_glu,
             b_glu=b_glu, w_out=w_out, g_ffn=g_ffn, w_router=w_router, b_router=b_router, w_gu=w_gu, b_gu=b_gu,
             w_down=w_down, b_down=b_down)
    mod = _ada(jnp.concatenate([c_prompt, c_sample], axis=0), w_ada, b_ada)
    disc = _s5disc(s5_lam_re, s5_lam_im, s5_log_step)
    tabs_p = _rope_tables(jnp.arange(T, dtype=jnp.int32))
    tabs_s = _rope_tables(jnp.full((DB,), past_len, jnp.int32))
    caches = (cache_kv_latent, jnp.transpose(cache_k_rope, (0, 1, 3, 2)),
              jnp.transpose(cache_sb_k, (0, 1, 3, 4, 2)).reshape(L, n_pool, SB_WIDTH, PAGE_SIZE),
              jnp.transpose(cache_sb_v, (0, 1, 3, 4, 2)).reshape(L, n_pool, SB_WIDTH, PAGE_SIZE))
    xp = x_prompt.reshape(B * T, D_MODEL)
    xs = x_sample.reshape(DB, D_MODEL)
    p_rows, s_rows = [], []
    for l in range(L):
        lw = _layer_weights(l, p, disc)
        mods_p = _split_mod(mod[l, :B].reshape(B, 1, N_MOD * D_MODEL))
        mods_s = _split_mod(mod[l, B:].reshape(1, DB, N_MOD * D_MODEL))
        xp, st_p = _prompt_layer(xp, mods_p, tabs_p, lw, B=B, T=T, l=l)
        p_rows.append(st_p)
        h0 = (state_s5_re[l].reshape(DB, S5_LANES), state_s5_im[l].reshape(DB, S5_LANES))
        xs, st_s = _sample_layer(xs, mods_s, tabs_s, lw, caches, h0, page_table, p, l=l, DB=DB)
        s_rows.append(st_s)
    p_out = [jnp.stack(a) for a in zip(*p_rows)]
    s_out = [jnp.stack(a) for a in zip(*s_rows)]
    return (xp.reshape(B, T, D_MODEL), xs.reshape(DB, 1, D_MODEL), *p_out, *s_out)
```

```python
import functools

import jax
import jax.numpy as jnp
import numpy as np
from jax import lax
from jax.experimental import pallas as pl
from jax.experimental.pallas import tpu as pltpu
from jax.experimental.pallas import tpu_sc as plsc

F32 = jnp.float32
BF16 = jnp.bfloat16

D_MODEL = 1024
PAGE_SIZE = 128
MLA_HEADS = 6
MLA_NOPE = 64
MLA_ROPE = 32
MLA_V = 64
MLA_Q_LORA = 256
MLA_KV_LORA = 128
ROPE_THETA = 10000.0
S5_CH = 16
S5_GROUPS = 24
S5_STATE = 64
S5_WIDTH = S5_GROUPS * S5_CH
S5_LANES = S5_GROUPS * S5_STATE
SB_HEADS = 4
SB_HEAD_DIM = 64
SB_WIDTH = SB_HEADS * SB_HEAD_DIM
N_EXPERTS = 32
TOP_K = 4
D_FF = 1024
SWIGLU_LIMIT = 7.0
SWIGLU_ALPHA = 1.702
N_MOD = 6
EPS = 1e-6

LANE = 128
HEAD_PAD = 128
S5_BLK_GROUPS = LANE // S5_CH
S5_NBLK = S5_GROUPS // S5_BLK_GROUPS
S5_BLK_LANES = S5_BLK_GROUPS * S5_STATE
ROUTER_PAD = 128
NEG_BIG = -1e30
VMEM_LIMIT = 56 * 1024 * 1024

_C_Q, _C_KV, _C_KPE, _C_U, _C_SQ, _C_SK, _C_SV, _C_END = 0, 256, 384, 512, 896, 1152, 1408, 1664


def _cparams(sem):
    return pltpu.CompilerParams(dimension_semantics=sem, vmem_limit_bytes=VMEM_LIMIT)


def _dot(a, b):
    return jnp.dot(a, b, preferred_element_type=F32)


def _dot_nt(a, b):
    return lax.dot_general(a, b, (((1,), (1,)), ((), ())), preferred_element_type=F32)


def _rms(x, g):
    return x * lax.rsqrt(jnp.mean(x * x, axis=-1, keepdims=True) + EPS) * g


def _rope_lanes(x, cos, sin_a, sin_b):
    return x * cos + pltpu.roll(x, HEAD_PAD - MLA_ROPE // 2, 1) * sin_a + pltpu.roll(x, MLA_ROPE // 2, 1) * sin_b


def _ada_body(c_ref, w_ref, b_ref, o_ref):
    c = c_ref[...]
    s = (c * jax.nn.sigmoid(c)).astype(BF16)
    o_ref[0] = _dot(s, w_ref[0].astype(BF16)) + b_ref[0]


def _ada(c_all, w_ada, b_ada):
    L = w_ada.shape[0]
    R = c_all.shape[0]
    ncol = N_MOD * D_MODEL // D_MODEL
    return pl.pallas_call(
        _ada_body,
        grid=(L, ncol),
        in_specs=[pl.BlockSpec((R, D_MODEL), lambda l, j: (0, 0)),
                  pl.BlockSpec((1, D_MODEL, D_MODEL), lambda l, j: (l, 0, j)),
                  pl.BlockSpec((1, 1, D_MODEL), lambda l, j: (l, 0, j))],
        out_specs=pl.BlockSpec((1, R, D_MODEL), lambda l, j: (l, 0, j)),
        out_shape=jax.ShapeDtypeStruct((L, R, N_MOD * D_MODEL), F32),
        compiler_params=_cparams(("parallel", "parallel")),
    )(c_all, w_ada, b_ada.reshape(L, 1, N_MOD * D_MODEL))


def _s5disc_body(lr_ref, li_ref, ls_ref, ar_ref, ai_ref, fr_ref, fi_ref):
    lr, li = lr_ref[...], li_ref[...]
    dt = jnp.exp(ls_ref[...])
    mag = jnp.exp(lr * dt)
    a_re, a_im = mag * jnp.cos(li * dt), mag * jnp.sin(li * dt)
    nr, ni = a_re - 1.0, a_im
    den = lr * lr + li * li
    ar_ref[...] = a_re
    ai_ref[...] = a_im
    fr_ref[...] = (nr * lr + ni * li) / den
    fi_ref[...] = (ni * lr - nr * li) / den


def _s5disc(lam_re, lam_im, log_step):
    L, G, N = lam_re.shape
    R = L * G
    sds = jax.ShapeDtypeStruct((R, N), F32)
    outs = pl.pallas_call(_s5disc_body, out_shape=(sds, sds, sds, sds))(
        lam_re.reshape(R, N), lam_im.reshape(R, N), log_step.reshape(R, 1))
    return [o.reshape(L, G, N) for o in outs]


def _proj_body(x_ref, sc_ref, sh_ref, gmix_ref, w1_ref, gql_ref, wuq_ref, gkv_ref, wuk_ref, wuv_ref,
               gq_ref, gk_ref, gkr_ref, cos_ref, sa_ref, sb_ref,
               q_ref, k_ref, v_ref, ckv_ref, kpe_ref, u_ref, sq_ref, sk16_ref, sv16_ref, sk_ref, sv_ref):
    x = x_ref[...]
    hn = _rms(x, gmix_ref[...]) * (1.0 + sc_ref[0]) + sh_ref[0]
    z = _dot(hn.astype(BF16), w1_ref[...])
    tm = x.shape[0]
    lane = lax.broadcasted_iota(jnp.int32, (tm, HEAD_PAD), 1)
    nope = lane < MLA_NOPE
    cos, sin_a, sin_b = cos_ref[...], sa_ref[...], sb_ref[...]

    ckv = _rms(z[:, _C_KV:_C_KPE], gkv_ref[...])
    ckv_ref[...] = ckv
    kraw = z[:, _C_KPE:_C_U]
    kr = kraw * lax.rsqrt(jnp.sum(kraw * kraw, axis=-1, keepdims=True) * (1.0 / MLA_ROPE) + EPS) * gkr_ref[...]
    kr = _rope_lanes(kr, cos, sin_a, sin_b)
    kpe_ref[...] = pltpu.roll(kr, HEAD_PAD - MLA_NOPE, 1)[:, :MLA_ROPE]

    qn = _rms(z[:, _C_Q:_C_KV], gql_ref[...]).astype(BF16)
    qa = _dot(qn, wuq_ref[...])
    cb = ckv.astype(BF16)
    ka = _dot(cb, wuk_ref[...])
    for h in range(MLA_HEADS):
        sl = slice(h * HEAD_PAD, (h + 1) * HEAD_PAD)
        qh = qa[:, sl]
        q2 = qh * qh
        ss_n = jnp.sum(jnp.where(nope, q2, 0.0), axis=-1, keepdims=True) * (1.0 / MLA_NOPE)
        ss_r = jnp.sum(jnp.where(nope, 0.0, q2), axis=-1, keepdims=True) * (1.0 / MLA_ROPE)
        qh = qh * jnp.where(nope, lax.rsqrt(ss_n + EPS), lax.rsqrt(ss_r + EPS)) * gq_ref[...]
        q_ref[:, sl] = _rope_lanes(qh, cos, sin_a, sin_b).astype(q_ref.dtype)
        kh = ka[:, sl]
        kh = kh * lax.rsqrt(jnp.sum(kh * kh, axis=-1, keepdims=True) * (1.0 / MLA_NOPE) + EPS) * gk_ref[...]
        k_ref[:, sl] = (kh + kr).astype(k_ref.dtype)
    v_ref[...] = _dot(cb, wuv_ref[...]).astype(v_ref.dtype)

    u_ref[...] = z[:, _C_U:_C_SQ]
    sq_ref[...] = (z[:, _C_SQ:_C_SK] * (SB_HEAD_DIM ** -0.5)).astype(BF16)
    sk = z[:, _C_SK:_C_SV]
    sv = z[:, _C_SV:_C_END]
    sk_ref[...] = sk
    sv_ref[...] = sv
    sk16_ref[...] = sk.astype(BF16)
    sv16_ref[...] = sv.astype(BF16)


def _proj(x, sc, sh, tabs, lw, *, B, T, tm, qk_dtype):
    N = B * T
    nT = T // tm
    mt = sc.shape[1]
    row = lambda b, t: (b * nT + t, 0)
    full = lambda b, t: (0, 0)
    if mt == 1:
        mod_spec = pl.BlockSpec((1, 1, D_MODEL), lambda b, t: (b, 0, 0))
    else:
        mod_spec = pl.BlockSpec((1, tm, D_MODEL), lambda b, t: (b, t, 0))
    tab_spec = pl.BlockSpec((tm, HEAD_PAD), lambda b, t: (t, 0))

    def wspec(a):
        return pl.BlockSpec(a.shape, full)

    weights = [lw['g_mix'], lw['w1'], lw['g_q_lat'], lw['wuq'], lw['g_kv_lat'], lw['wuk'], lw['wuv'],
               lw['gq'], lw['gk'], lw['gkr']]
    in_specs = ([pl.BlockSpec((tm, D_MODEL), row), mod_spec, mod_spec] + [wspec(w) for w in weights[:1]]
                + [wspec(w) for w in weights[1:]] + [tab_spec] * 3)
    QW = MLA_HEADS * HEAD_PAD
    VW = MLA_HEADS * MLA_V
    out_shape = (
        jax.ShapeDtypeStruct((N, QW), qk_dtype), jax.ShapeDtypeStruct((N, QW), qk_dtype),
        jax.ShapeDtypeStruct((N, VW), BF16),
        jax.ShapeDtypeStruct((N, MLA_KV_LORA), F32), jax.ShapeDtypeStruct((N, MLA_ROPE), F32),
        jax.ShapeDtypeStruct((T, B * S5_WIDTH), F32),
        jax.ShapeDtypeStruct((N, SB_WIDTH), BF16), jax.ShapeDtypeStruct((N, SB_WIDTH), BF16),
        jax.ShapeDtypeStruct((N, SB_WIDTH), BF16),
        jax.ShapeDtypeStruct((N, SB_WIDTH), F32), jax.ShapeDtypeStruct((N, SB_WIDTH), F32))
    out_specs = (
        pl.BlockSpec((tm, QW), row), pl.BlockSpec((tm, QW), row), pl.BlockSpec((tm, VW), row),
        pl.BlockSpec((tm, MLA_KV_LORA), row), pl.BlockSpec((tm, MLA_ROPE), row),
        pl.BlockSpec((tm, S5_WIDTH), lambda b, t: (t, b)),
        pl.BlockSpec((tm, SB_WIDTH), row), pl.BlockSpec((tm, SB_WIDTH), row), pl.BlockSpec((tm, SB_WIDTH), row),
        pl.BlockSpec((tm, SB_WIDTH), row), pl.BlockSpec((tm, SB_WIDTH), row))
    return pl.pallas_call(
        _proj_body, grid=(B, nT), in_specs=in_specs, out_specs=out_specs, out_shape=out_shape,
        compiler_params=_cparams(("parallel", "parallel")),
    )(x, sc, sh, *weights, *tabs)


def _mla_body(q_ref, k_ref, v_ref, o_ref, m_ref, l_ref, acc_ref, *, tq, scale):
    qi = pl.program_id(2)
    m_ref[...] = jnp.full(m_ref.shape, -jnp.inf, F32)
    l_ref[...] = jnp.zeros(l_ref.shape, F32)
    acc_ref[...] = jnp.zeros(acc_ref.shape, F32)
    q = q_ref[...]

    def block(kb, masked):
        r0 = pl.multiple_of(kb * tq, tq)
        k = k_ref[pl.ds(r0, tq), :]
        v = v_ref[pl.ds(r0, tq), :]
        for hh in range(2):
            sl = slice(hh * HEAD_PAD, (hh + 1) * HEAD_PAD)
            s = _dot_nt(q[:, sl], k[:, sl]) * scale
            if masked:
                rowi = lax.broadcasted_iota(jnp.int32, s.shape, 0)
                coli = lax.broadcasted_iota(jnp.int32, s.shape, 1)
                s = jnp.where(coli <= rowi, s, -jnp.inf)
            m_prev = m_ref[hh]
            m_new = jnp.maximum(m_prev, jnp.max(s, axis=-1, keepdims=True))
            alpha = jnp.exp(m_prev - m_new)
            p = jnp.exp(s - m_new[:, :1])
            l_ref[hh] = alpha * l_ref[hh] + jnp.sum(p, axis=-1, keepdims=True)
            acc_ref[hh] = alpha * acc_ref[hh] + _dot(p.astype(BF16), v)
            m_ref[hh] = m_new

    def loop_body(kb, carry):
        block(kb, False)
        return carry

    lax.fori_loop(0, qi, loop_body, 0)
    block(qi, True)
    lane = lax.broadcasted_iota(jnp.int32, (tq, LANE), 1)
    o_ref[...] = jnp.where(lane < MLA_V, acc_ref[0] / l_ref[0], acc_ref[1] / l_ref[1]).astype(o_ref.dtype)


def _mla_prompt(q, k, v, *, B, T, tq):
    nq = T // tq
    npair = MLA_HEADS // 2
    scale = (MLA_NOPE + MLA_ROPE) ** -0.5
    return pl.pallas_call(
        functools.partial(_mla_body, tq=tq, scale=scale),
        grid=(B, npair, nq),
        in_specs=[pl.BlockSpec((tq, 2 * HEAD_PAD), lambda b, h, i: (b * nq + i, h)),
                  pl.BlockSpec((T, 2 * HEAD_PAD), lambda b, h, i: (b, h)),
                  pl.BlockSpec((T, 2 * MLA_V), lambda b, h, i: (b, h))],
        out_specs=pl.BlockSpec((tq, 2 * MLA_V), lambda b, h, i: (b * nq + i, h)),
        out_shape=jax.ShapeDtypeStruct((B * T, MLA_HEADS * MLA_V), BF16),
        scratch_shapes=[pltpu.VMEM((2, tq, LANE), F32), pltpu.VMEM((2, tq, LANE), F32),
                        pltpu.VMEM((2, tq, LANE), F32)],
        compiler_params=_cparams(("parallel", "parallel", "parallel")),
    )(q, k, v)


def _log_sigmoid_pair(z):
    lb = jnp.minimum(z, 0.0) - jnp.log(1.0 + jnp.exp(-jnp.abs(z)))
    return lb, lb - z


def _split_bf16(x):
    hi = x.astype(BF16)
    return hi, (x - hi.astype(F32)).astype(BF16)


def _sb_body(q_ref, k_ref, v_ref, uo_ref, o_ref, acc_ref, c_ref, *, tq):
    qi = pl.program_id(1)
    acc_ref[...] = jnp.zeros(acc_ref.shape, F32)
    c_ref[...] = jnp.zeros(c_ref.shape, F32)
    lane = lax.broadcasted_iota(jnp.int32, (tq, LANE), 1)
    qh = []
    for pr in range(SB_HEADS // 2):
        q = q_ref[:, pr * LANE:(pr + 1) * LANE]
        zero = jnp.zeros_like(q)
        qh += [jnp.where(lane < SB_HEAD_DIM, q, zero), jnp.where(lane < SB_HEAD_DIM, zero, q)]
    uo = uo_ref[...]

    def block(kb, masked):
        r0 = pl.multiple_of(kb * tq, tq)
        for h in range(SB_HEADS):
            pr = h // 2
            k = k_ref[pl.ds(r0, tq), pr * LANE:(pr + 1) * LANE]
            v = v_ref[pl.ds(r0, tq), pr * LANE:(pr + 1) * LANE]
            z = _dot_nt(qh[h], k)
            lb, lk = _log_sigmoid_pair(z)
            if masked:
                rowi = lax.broadcasted_iota(jnp.int32, z.shape, 0)
                coli = lax.broadcasted_iota(jnp.int32, z.shape, 1)
                valid = coli < rowi
                lb = jnp.where(valid, lb, -jnp.inf)
                lk = jnp.where(valid, lk, 0.0)
            hi, lo = _split_bf16(lk)
            r = _dot(hi, uo) + _dot(lo, uo)
            w = jnp.exp(lb + r[:, :tq] + c_ref[h])
            acc_ref[h] = acc_ref[h] + _dot(w.astype(BF16), v)
            c_ref[h] = c_ref[h] + r[:, tq:]

    block(qi, True)

    def loop_body(j, carry):
        block(qi - 1 - j, False)
        return carry

    lax.fori_loop(0, qi, loop_body, 0)
    for pr in range(SB_HEADS // 2):
        o_ref[:, pr * LANE:(pr + 1) * LANE] = jnp.where(
            lane < SB_HEAD_DIM, acc_ref[2 * pr], acc_ref[2 * pr + 1]).astype(o_ref.dtype)


def _suffix_ones(n):
    j = np.arange(n)[:, None]
    s = np.arange(n)[None, :]
    return jnp.asarray(np.concatenate([(j > s), np.ones((n, n), bool)], axis=1), BF16)


def _sb_prompt(q, k, v, *, B, T, tq):
    nq = T // tq
    return pl.pallas_call(
        functools.partial(_sb_body, tq=tq),
        grid=(B, nq),
        in_specs=[pl.BlockSpec((tq, SB_WIDTH), lambda b, i: (b * nq + i, 0)),
                  pl.BlockSpec((T, SB_WIDTH), lambda b, i: (b, 0)),
                  pl.BlockSpec((T, SB_WIDTH), lambda b, i: (b, 0)),
                  pl.BlockSpec((tq, 2 * tq), lambda b, i: (0, 0))],
        out_specs=pl.BlockSpec((tq, SB_WIDTH), lambda b, i: (b * nq + i, 0)),
        out_shape=jax.ShapeDtypeStruct((B * T, SB_WIDTH), BF16),
        scratch_shapes=[pltpu.VMEM((SB_HEADS, tq, LANE), F32), pltpu.VMEM((SB_HEADS, tq, tq), F32)],
        compiler_params=_cparams(("parallel", "parallel")),
    )(q, k, v, _suffix_ones(tq))


def _s5_body(u_ref, h0r_ref, h0i_ref, ar_ref, ai_ref, bbr_ref, bbi_ref, cr_ref, ci_ref, d_ref, wg_ref, bg_ref,
             o_ref, sr_ref, si_ref, hr_s, hi_s, hsr, hsi, *, B, Tc):
    @pl.when(pl.program_id(0) == 0)
    def _():
        hr_s[...] = h0r_ref[...]
        hi_s[...] = h0i_ref[...]

    u = u_ref[...]
    ub = u.astype(BF16)
    for j in range(S5_NBLK):
        uj = ub[:, j * LANE:(j + 1) * LANE]
        hsr[:, j * S5_BLK_LANES:(j + 1) * S5_BLK_LANES] = _dot(uj, bbr_ref[j])
        hsi[:, j * S5_BLK_LANES:(j + 1) * S5_BLK_LANES] = _dot(uj, bbi_ref[j])
    ar, ai = ar_ref[...], ai_ref[...]

    def step(t, carry):
        hr, hi = carry
        r0 = pl.multiple_of(t * B, B)
        nr = ar * hr - ai * hi + hsr[pl.ds(r0, B), :]
        ni = ar * hi + ai * hr + hsi[pl.ds(r0, B), :]
        hsr[pl.ds(r0, B), :] = nr
        hsi[pl.ds(r0, B), :] = ni
        return nr, ni

    hr, hi = lax.fori_loop(0, Tc, step, (hr_s[...], hi_s[...]))
    hr_s[...] = hr
    hi_s[...] = hi
    sr_ref[...] = hr
    si_ref[...] = hi

    ys = []
    for j in range(S5_NBLK):
        sl = slice(j * S5_BLK_LANES, (j + 1) * S5_BLK_LANES)
        ys.append(_dot(hsr[:, sl].astype(BF16), cr_ref[j]) - _dot(hsi[:, sl].astype(BF16), ci_ref[j]))
    y = jnp.concatenate(ys, axis=1) + d_ref[...] * u
    y = jax.nn.gelu(y)
    gate = _dot(y.astype(BF16), wg_ref[...]) + bg_ref[...]
    o_ref[...] = (y * jax.nn.sigmoid(gate)).astype(o_ref.dtype)


def _s5(u, h0r, h0i, lw, *, B, T, Tc):
    rows = Tc * B
    full2 = lambda i: (0, 0)
    full3 = lambda i: (0, 0, 0)
    st = jax.ShapeDtypeStruct((B, S5_LANES), F32)
    return pl.pallas_call(
        functools.partial(_s5_body, B=B, Tc=Tc),
        grid=(T // Tc,),
        in_specs=[pl.BlockSpec((rows, S5_WIDTH), lambda i: (i, 0)),
                  pl.BlockSpec((B, S5_LANES), full2), pl.BlockSpec((B, S5_LANES), full2),
                  pl.BlockSpec((1, S5_LANES), full2), pl.BlockSpec((1, S5_LANES), full2),
                  pl.BlockSpec((S5_NBLK, LANE, S5_BLK_LANES), full3),
                  pl.BlockSpec((S5_NBLK, LANE, S5_BLK_LANES), full3),
                  pl.BlockSpec((S5_NBLK, S5_BLK_LANES, LANE), full3),
                  pl.BlockSpec((S5_NBLK, S5_BLK_LANES, LANE), full3),
                  pl.BlockSpec((1, S5_WIDTH), full2), pl.BlockSpec((S5_WIDTH, S5_WIDTH), full2),
                  pl.BlockSpec((1, S5_WIDTH), full2)],
        out_specs=(pl.BlockSpec((rows, S5_WIDTH), lambda i: (i, 0)),
                   pl.BlockSpec((B, S5_LANES), full2), pl.BlockSpec((B, S5_LANES), full2)),
        out_shape=(jax.ShapeDtypeStruct((T * B, S5_WIDTH), BF16), st, st),
        scratch_shapes=[pltpu.VMEM((B, S5_LANES), F32), pltpu.VMEM((B, S5_LANES), F32),
                        pltpu.VMEM((rows, S5_LANES), F32), pltpu.VMEM((rows, S5_LANES), F32)],
        compiler_params=_cparams(("arbitrary",)),
    )(u, h0r, h0i, lw['a_re'], lw['a_im'], lw['bb_re'], lw['bb_im'], lw['cc_re'], lw['cc_im'],
      lw['s5_d'], lw['w_glu'], lw['b_glu'])


def _out_body(x_ref, oa_ref, ob_ref, oc_ref, gt_ref, sc_ref, sh_ref, wo_ref, gffn_ref, wrh_ref, wrl_ref, br_ref,
              x1_ref, h2_ref, ti_ref, tg_ref):
    wa = MLA_HEADS * MLA_V
    o = (_dot(oa_ref[...], wo_ref[0:wa, :]) + _dot(ob_ref[...], wo_ref[wa:wa + S5_WIDTH, :])
         + _dot(oc_ref[...], wo_ref[wa + S5_WIDTH:, :]))
    x1 = x_ref[...] + gt_ref[0] * o
    x1_ref[...] = x1
    h2 = _rms(x1, gffn_ref[...]) * (1.0 + sc_ref[0]) + sh_ref[0]
    hb = h2.astype(BF16).astype(F32)
    h2_ref[...] = (pltpu.bitcast(hb[:, :D_MODEL // 2], jnp.uint32)
                   | (pltpu.bitcast(hb[:, D_MODEL // 2:], jnp.uint32) >> 16))
    hi, lo = _split_bf16(h2)
    logits = _dot(hi, wrh_ref[...]) + _dot(hi, wrl_ref[...]) + _dot(lo, wrh_ref[...]) + br_ref[...]
    lane = lax.broadcasted_iota(jnp.int32, logits.shape, 1)
    lanef = lane.astype(F32)
    vals, idxs = [], []
    for _ in range(TOP_K):
        m = jnp.max(logits, axis=-1, keepdims=True)
        idx = jnp.min(jnp.where(logits == m, lanef, float(ROUTER_PAD)), axis=-1, keepdims=True).astype(jnp.int32)
        vals.append(m)
        idxs.append(idx)
        logits = jnp.where(lane == idx, -jnp.inf, logits)
    es = [jnp.exp(v - vals[0]) for v in vals]
    den = es[0] + es[1] + es[2] + es[3]
    ti = jnp.zeros(lane.shape, jnp.int32)
    tg = jnp.zeros(lane.shape, F32)
    for kk in range(TOP_K):
        ti = jnp.where(lane == kk, idxs[kk], ti)
        tg = jnp.where(lane == kk, es[kk] / den, tg)
    ti_ref[...] = ti
    tg_ref[...] = tg


def _outproj(x, oa, ob, oc, gt, sc, sh, lw, *, B, T, tm):
    N = B * T
    nT = T // tm
    mt = gt.shape[1]
    row = lambda b, t: (b * nT + t, 0)
    full = lambda b, t: (0, 0)
    if mt == 1:
        mod_spec = pl.BlockSpec((1, 1, D_MODEL), lambda b, t: (b, 0, 0))
    else:
        mod_spec = pl.BlockSpec((1, tm, D_MODEL), lambda b, t: (b, t, 0))
    weights = [lw['w_out'], lw['g_ffn'], lw['wr_hi'], lw['wr_lo'], lw['b_router']]
    return pl.pallas_call(
        _out_body, grid=(B, nT),
        in_specs=[pl.BlockSpec((tm, D_MODEL), row), pl.BlockSpec((tm, MLA_HEADS * MLA_V), row),
                  pl.BlockSpec((tm, S5_WIDTH), lambda b, t: (t, b)), pl.BlockSpec((tm, SB_WIDTH), row),
                  mod_spec, mod_spec, mod_spec] + [pl.BlockSpec(w.shape, full) for w in weights],
        out_specs=(pl.BlockSpec((tm, D_MODEL), row), pl.BlockSpec((tm, D_MODEL // 2), row),
                   pl.BlockSpec((tm, ROUTER_PAD), row), pl.BlockSpec((tm, ROUTER_PAD), row)),
        out_shape=(jax.ShapeDtypeStruct((N, D_MODEL), F32), jax.ShapeDtypeStruct((N, D_MODEL // 2), jnp.uint32),
                   jax.ShapeDtypeStruct((N, ROUTER_PAD), jnp.int32), jax.ShapeDtypeStruct((N, ROUTER_PAD), F32)),
        compiler_params=_cparams(("parallel", "parallel")),
    )(x, oa, ob, oc, gt, sc, sh, *weights)


def _moe_body(be_ref, nu_ref, x_ref, wg_ref, wu_ref, wd_ref, bg_ref, bu_ref, bd_ref, after_ref, o_ref, wd16_ref):
    del after_ref
    i = pl.program_id(0)
    used = i < nu_ref[0]

    @pl.when(used & ((i == 0) | (be_ref[i] != be_ref[jnp.maximum(i - 1, 0)])))
    def _():
        wd16_ref[...] = wd_ref[0, 0].astype(BF16)

    @pl.when(used)
    def _():
        w = x_ref[...]
        x = jnp.concatenate([pltpu.bitcast(w & jnp.uint32(0xFFFF0000), F32), pltpu.bitcast(w << 16, F32)],
                            axis=1).astype(BF16)
        g = jnp.minimum(_dot(x, wg_ref[0]) + bg_ref[0], SWIGLU_LIMIT)
        up = jnp.clip(_dot(x, wu_ref[0]) + bu_ref[0], -SWIGLU_LIMIT, SWIGLU_LIMIT)
        act = (up + 1.0) * (g * jax.nn.sigmoid(SWIGLU_ALPHA * g))
        o_ref[...] = _dot(act.astype(BF16), wd16_ref[...]) + bd_ref[0]

    @pl.when(i >= nu_ref[0])
    def _():
        o_ref[...] = jnp.zeros(o_ref.shape, F32)


def _moe(block_expert, n_used, xs, lw, after, *, bm, layer):
    n_slots = xs.shape[0]
    n_blocks = n_slots // bm
    wsel = lambda i, be, nu: (be[i], 0, 0)
    row = lambda i, be, nu: (i, 0)
    return pl.pallas_call(
        _moe_body,
        grid_spec=pltpu.PrefetchScalarGridSpec(
            num_scalar_prefetch=2, grid=(n_blocks,),
            in_specs=[pl.BlockSpec((bm, D_MODEL // 2), row),
                      pl.BlockSpec((1, D_MODEL, D_FF), wsel), pl.BlockSpec((1, D_MODEL, D_FF), wsel),
                      pl.BlockSpec((1, 1, D_FF, D_MODEL), lambda i, be, nu: (layer, be[i], 0, 0)),
                      pl.BlockSpec((1, 1, D_FF), wsel), pl.BlockSpec((1, 1, D_FF), wsel),
                      pl.BlockSpec((1, 1, D_MODEL), wsel), pl.BlockSpec(memory_space=pl.ANY)],
            out_specs=pl.BlockSpec((bm, D_MODEL), row),
            scratch_shapes=[pltpu.VMEM((D_FF, D_MODEL), BF16)]),
        out_shape=jax.ShapeDtypeStruct((n_slots, D_MODEL), F32),
        compiler_params=_cparams(("arbitrary",)),
    )(block_expert, n_used, xs, lw['w_gate'], lw['w_up'], lw['w_down'], lw['b_gate'], lw['b_up'], lw['b_down'],
      after)


def _route(topi, bm):
    n = topi.shape[0]
    tk = n * TOP_K
    n_blocks = -(-tk // bm) + N_EXPERTS
    n_fill = n_blocks * bm - tk
    flat_e = topi[:, :TOP_K].reshape(tk)
    counts = jnp.sum((flat_e[:, None] == jnp.arange(N_EXPERTS, dtype=jnp.int32)[None, :]).astype(jnp.int32), axis=0)
    blocks_per_e = (counts + bm - 1) // bm
    block_end = jnp.cumsum(blocks_per_e)
    pads = blocks_per_e * bm - counts
    f = jnp.arange(n_fill, dtype=jnp.int32)
    f_e, f_r = f // bm, f % bm
    f_pads = jnp.concatenate([jnp.repeat(pads, bm), jnp.zeros((n_fill - N_EXPERTS * bm,), jnp.int32)])
    keys = jnp.concatenate([2 * flat_e, jnp.where(f_r < f_pads, 2 * f_e + 1, 2 * N_EXPERTS)])
    order = jnp.argsort(keys).astype(jnp.int32)
    slot_tok = jnp.where(order < tk, order // TOP_K, 0)
    dest = jnp.argsort(order).astype(jnp.int32)[:tk]
    blk = jnp.arange(n_blocks, dtype=jnp.int32)
    block_expert = jnp.minimum(jnp.sum((block_end[None, :] <= blk[:, None]).astype(jnp.int32), axis=1),
                               N_EXPERTS - 1)
    return slot_tok, dest, block_expert, block_end[-1:].astype(jnp.int32)


SC_WINDOW_BYTES = 128 * 1024


def _gather_rows(x, idx):
    n_idx = idx.shape[0]
    d = x.shape[1]
    window = SC_WINDOW_BYTES // (d * x.dtype.itemsize)
    info = plsc.get_sparse_core_info()
    nc, ns = info.num_cores, info.num_subcores
    per_w = n_idx // (nc * ns)
    if n_idx % (nc * ns * window) != 0:
        return jnp.take(x, idx, axis=0)
    mesh = plsc.VectorSubcoreMesh(core_axis_name="c", subcore_axis_name="s")

    @functools.partial(
        pl.kernel, out_type=jax.ShapeDtypeStruct((n_idx, d), x.dtype), mesh=mesh,
        scratch_types=[pltpu.VMEM((window,), jnp.int32), pltpu.VMEM((window, d), x.dtype),
                       pltpu.SemaphoreType.DMA])
    def gather_kernel(x_hbm, i_hbm, o_hbm, idx_v, rows_v, sem):
        base = (lax.axis_index("s") * nc + lax.axis_index("c")) * per_w

        @pl.loop(0, per_w // window)
        def _(j):
            off = base + j * window
            pltpu.sync_copy(i_hbm.at[pl.ds(off, window)], idx_v)
            pltpu.async_copy(x_hbm.at[idx_v], rows_v, sem).wait()
            pltpu.sync_copy(rows_v, o_hbm.at[pl.ds(off, window)])

    return gather_kernel(x, idx)


def _combine_body(x_ref, y_ref, g_ref, gt_ref, after_ref, o_ref):
    del after_ref
    g = g_ref[...]
    y = y_ref[0] * g[:, 0:1]
    for kk in range(1, TOP_K):
        y = y + y_ref[kk] * g[:, kk:kk + 1]
    o_ref[...] = x_ref[...] + gt_ref[0] * y


def _combine(x1, yk, topg, gt, after, *, B, T, tm):
    nT = T // tm
    mt = gt.shape[1]
    row = lambda b, t: (b * nT + t, 0)
    if mt == 1:
        mod_spec = pl.BlockSpec((1, 1, D_MODEL), lambda b, t: (b, 0, 0))
    else:
        mod_spec = pl.BlockSpec((1, tm, D_MODEL), lambda b, t: (b, t, 0))
    return pl.pallas_call(
        _combine_body, grid=(B, nT),
        in_specs=[pl.BlockSpec((tm, D_MODEL), row),
                  pl.BlockSpec((TOP_K, tm, D_MODEL), lambda b, t: (0, b * nT + t, 0)),
                  pl.BlockSpec((tm, ROUTER_PAD), row), mod_spec, pl.BlockSpec(memory_space=pl.ANY)],
        out_specs=pl.BlockSpec((tm, D_MODEL), row),
        out_shape=jax.ShapeDtypeStruct(x1.shape, F32),
        compiler_params=_cparams(("parallel", "parallel")),
    )(x1, yk, topg, gt, after)


def _page_copies(pt_ref, b, c, slot, srcs, bufs, sems, keys_on_lanes, *, layer, G):
    out = []
    for g in range(G):
        page = pt_ref[b, c * G + g]
        for src, buf, sem, on_lanes in zip(srcs, bufs, sems, keys_on_lanes):
            keys = pl.ds(g * PAGE_SIZE, PAGE_SIZE)
            dst = buf.at[slot, :, keys] if on_lanes else buf.at[slot, keys]
            out.append(pltpu.make_async_copy(src.at[layer, page], dst, sem.at[slot]))
    return out


def _mla_dec_body(pt_ref, qm_ref, qpe_ref, q8_ref, k8_ref, cn_ref, wukt_ref, wuv_ref, kv_hbm, kr_hbm, after_ref,
                  o_ref, kvbuf, krbuf, sem_kv, sem_kr, m_ref, l_ref, pc_ref, lhs_ref, *, layer, G, nc, nb, scale):
    del after_ref
    b = pl.program_id(0)
    c = pl.program_id(1)
    step = b * nc + c
    slot = lax.rem(step, 2)
    copies = functools.partial(_page_copies, pt_ref, srcs=(kv_hbm, kr_hbm), bufs=(kvbuf, krbuf),
                               sems=(sem_kv, sem_kr), keys_on_lanes=(False, True), layer=layer, G=G)

    @pl.when(step == 0)
    def _():
        for cp in copies(0, 0, 0):
            cp.start()

    @pl.when(step + 1 < nb * nc)
    def _():
        nxt = step + 1
        for cp in copies(nxt // nc, lax.rem(nxt, nc), 1 - slot):
            cp.start()

    @pl.when(c == 0)
    def _():
        m_ref[...] = jnp.full(m_ref.shape, -jnp.inf, F32)
        l_ref[...] = jnp.zeros(l_ref.shape, F32)
        pc_ref[...] = jnp.zeros(pc_ref.shape, F32)
        nk = MLA_HEADS * MLA_NOPE
        lhs_ref[0:nk, :] = wukt_ref[...]
        qabs = _dot(qm_ref[0], wukt_ref[...])
        hi = qabs.astype(BF16).astype(F32)
        lhs_ref[nk:nk + 16, :] = jnp.concatenate([hi, qabs - hi], axis=0).astype(BF16)

    for cp in copies(b, c, slot):
        cp.wait()

    nk = MLA_HEADS * MLA_NOPE
    cb = kvbuf[slot].astype(BF16)
    res = _dot_nt(lhs_ref[...], cb)
    sn = res[nk:nk + 8, :] + res[nk + 8:nk + 16, :]
    ss = []
    for h in range(MLA_HEADS):
        kh = res[h * MLA_NOPE:(h + 1) * MLA_NOPE, :]
        ss.append(jnp.sum(kh * kh, axis=0, keepdims=True))
    ss = jnp.concatenate(ss + [jnp.zeros((8 - MLA_HEADS, res.shape[1]), F32)], axis=0)
    s = sn * lax.rsqrt(ss * (1.0 / MLA_NOPE) + EPS)
    s = (s + _dot(qpe_ref[0], krbuf[slot].astype(BF16))) * scale
    m_prev = m_ref[...]
    m_new = jnp.maximum(m_prev, jnp.max(s, axis=-1, keepdims=True))
    alpha = jnp.exp(m_prev - m_new)
    p = jnp.exp(s - m_new[:, :1])
    l_ref[...] = alpha * l_ref[...] + jnp.sum(p, axis=-1, keepdims=True)
    pc_ref[...] = alpha * pc_ref[...] + _dot(p.astype(BF16), cb)
    m_ref[...] = m_new

    @pl.when(c == nc - 1)
    def _():
        s_new = jnp.sum(q8_ref[0].astype(F32) * k8_ref[0].astype(F32), axis=-1, keepdims=True) * scale
        m_f = jnp.maximum(m_ref[...], s_new)
        a_old = jnp.exp(m_ref[...] - m_f)
        p_new = jnp.exp(s_new - m_f)
        l_f = a_old * l_ref[...] + p_new
        ctx = (a_old * pc_ref[...] + p_new * cn_ref[0]) / l_f
        full = _dot(ctx.astype(BF16), wuv_ref[...])
        rowi = lax.broadcasted_iota(jnp.int32, full.shape, 0)
        coli = lax.broadcasted_iota(jnp.int32, full.shape, 1)
        own = (coli >= rowi * MLA_V) & (coli < (rowi + 1) * MLA_V)
        o_ref[0] = jnp.sum(jnp.where(own, full, 0.0), axis=0, keepdims=True).astype(o_ref.dtype)


def _mla_decode(page_table, qm, qpe, q8, k8, cn, lw, cache_kv, cache_krt, after, *, layer):
    nb, n_pages = page_table.shape
    G = min(32, n_pages)
    nc = n_pages // G
    rows = G * PAGE_SIZE
    scale = (MLA_NOPE + MLA_ROPE) ** -0.5
    per_b = lambda b, c, pt: (b, 0, 0)
    full = lambda b, c, pt: (0, 0)
    return pl.pallas_call(
        functools.partial(_mla_dec_body, layer=layer, G=G, nc=nc, nb=nb, scale=scale),
        grid_spec=pltpu.PrefetchScalarGridSpec(
            num_scalar_prefetch=1, grid=(nb, nc),
            in_specs=[pl.BlockSpec((1, 8, MLA_HEADS * MLA_NOPE), per_b), pl.BlockSpec((1, 8, MLA_ROPE), per_b),
                      pl.BlockSpec((1, 8, HEAD_PAD), per_b), pl.BlockSpec((1, 8, HEAD_PAD), per_b),
                      pl.BlockSpec((1, 1, MLA_KV_LORA), per_b),
                      pl.BlockSpec(lw['wukt'].shape, full), pl.BlockSpec(lw['wuv'].shape, full),
                      pl.BlockSpec(memory_space=pl.ANY), pl.BlockSpec(memory_space=pl.ANY),
                      pl.BlockSpec(memory_space=pl.ANY)],
            out_specs=pl.BlockSpec((1, 1, MLA_HEADS * MLA_V), per_b),
            scratch_shapes=[pltpu.VMEM((2, rows, MLA_KV_LORA), F32), pltpu.VMEM((2, MLA_ROPE, rows), F32),
                            pltpu.SemaphoreType.DMA((2,)), pltpu.SemaphoreType.DMA((2,)),
                            pltpu.VMEM((8, LANE), F32), pltpu.VMEM((8, LANE), F32),
                            pltpu.VMEM((8, MLA_KV_LORA), F32),
                            pltpu.VMEM((MLA_HEADS * MLA_NOPE + 16, MLA_KV_LORA), BF16)]),
        out_shape=jax.ShapeDtypeStruct((nb, 1, MLA_HEADS * MLA_V), BF16),
        compiler_params=_cparams(("arbitrary", "arbitrary")),
    )(page_table, qm, qpe, q8, k8, cn, lw['wukt'], lw['wuv'], cache_kv, cache_krt, after)


def _sb_dec_body(pt_ref, q_ref, uo_ref, sfx_ref, k_hbm, v_hbm, after_ref, o_ref, kbuf, vbuf, sem_k, sem_v,
                 acc_ref, c_ref, *, layer, G, nc, nb):
    del after_ref
    b = pl.program_id(0)
    ci = pl.program_id(1)
    step = b * nc + ci
    slot = lax.rem(step, 2)
    copies = functools.partial(_page_copies, pt_ref, srcs=(k_hbm, v_hbm), bufs=(kbuf, vbuf),
                               sems=(sem_k, sem_v), keys_on_lanes=(True, True), layer=layer, G=G)

    @pl.when(step == 0)
    def _():
        for cp in copies(0, nc - 1, 0):
            cp.start()

    @pl.when(step + 1 < nb * nc)
    def _():
        nxt = step + 1
        for cp in copies(nxt // nc, nc - 1 - lax.rem(nxt, nc), 1 - slot):
            cp.start()

    @pl.when(ci == 0)
    def _():
        acc_ref[...] = jnp.zeros(acc_ref.shape, F32)
        c_ref[...] = jnp.zeros(c_ref.shape, F32)

    for cp in copies(b, nc - 1 - ci, slot):
        cp.wait()

    q = q_ref[0]
    kt = kbuf[slot].astype(BF16)
    vt = vbuf[slot].astype(BF16)
    page = lambda a, g: a[:, g * PAGE_SIZE:(g + 1) * PAGE_SIZE]
    z = jnp.concatenate([_dot(q, page(kt, g)) for g in range(G)], axis=0)
    lb, lk = _log_sigmoid_pair(z)
    hi, lo = _split_bf16(lk)
    r = _dot(hi, uo_ref[...]) + _dot(lo, uo_ref[...])
    th, tl = _split_bf16(r[:, PAGE_SIZE:])
    later = _dot(sfx_ref[...], th) + _dot(sfx_ref[...], tl)
    c = c_ref[...]
    w = jnp.exp(lb + r[:, :PAGE_SIZE] + later + jnp.concatenate([c] * G, axis=0))
    acc = acc_ref[...]
    for g in range(G):
        acc = acc + _dot_nt(w[g * 8:(g + 1) * 8, :].astype(BF16), page(vt, g))
    acc_ref[...] = acc
    c_ref[...] = c + later[0:8, :] + r[0:8, PAGE_SIZE:]

    @pl.when(ci == nc - 1)
    def _():
        acc = acc_ref[...]
        rowi = lax.broadcasted_iota(jnp.int32, acc.shape, 0)
        coli = lax.broadcasted_iota(jnp.int32, acc.shape, 1)
        own = (coli >= rowi * SB_HEAD_DIM) & (coli < (rowi + 1) * SB_HEAD_DIM)
        o_ref[0] = jnp.sum(jnp.where(own, acc, 0.0), axis=0, keepdims=True).astype(o_ref.dtype)


def _sb_decode(page_table, qs, cache_kt, cache_vt, after, *, layer):
    nb, n_pages = page_table.shape
    G = min(16, n_pages)
    nc = n_pages // G
    rows = G * PAGE_SIZE
    r = np.arange(8 * G)
    sfx = jnp.asarray((r[None, :] % 8 == r[:, None] % 8) & (r[None, :] // 8 > r[:, None] // 8), BF16)
    per_b = lambda b, c, pt: (b, 0, 0)
    full = lambda b, c, pt: (0, 0)
    return pl.pallas_call(
        functools.partial(_sb_dec_body, layer=layer, G=G, nc=nc, nb=nb),
        grid_spec=pltpu.PrefetchScalarGridSpec(
            num_scalar_prefetch=1, grid=(nb, nc),
            in_specs=[pl.BlockSpec((1, 8, SB_WIDTH), per_b), pl.BlockSpec((PAGE_SIZE, 2 * PAGE_SIZE), full),
                      pl.BlockSpec((8 * G, 8 * G), full),
                      pl.BlockSpec(memory_space=pl.ANY), pl.BlockSpec(memory_space=pl.ANY),
                      pl.BlockSpec(memory_space=pl.ANY)],
            out_specs=pl.BlockSpec((1, 1, SB_WIDTH), per_b),
            scratch_shapes=[pltpu.VMEM((2, SB_WIDTH, rows), F32), pltpu.VMEM((2, SB_WIDTH, rows), F32),
                            pltpu.SemaphoreType.DMA((2,)), pltpu.SemaphoreType.DMA((2,)),
                            pltpu.VMEM((8, SB_WIDTH), F32), pltpu.VMEM((8, LANE), F32)]),
        out_shape=jax.ShapeDtypeStruct((nb, 1, SB_WIDTH), BF16),
        compiler_params=_cparams(("arbitrary", "arbitrary")),
    )(page_table, qs, _suffix_ones(PAGE_SIZE), sfx, cache_kt, cache_vt, after)


def _rope_tables(pos):
    half = MLA_ROPE // 2
    inv = ROPE_THETA ** (-jnp.arange(half, dtype=F32) / half)
    ang = pos.astype(F32)[:, None] * inv
    cos, sin = jnp.cos(ang), jnp.sin(ang)
    n = pos.shape[0]
    z16 = jnp.zeros((n, half), F32)
    z32 = jnp.zeros((n, HEAD_PAD - MLA_NOPE - MLA_ROPE), F32)
    z64 = jnp.zeros((n, MLA_NOPE), F32)
    cos_t = jnp.concatenate([jnp.ones((n, MLA_NOPE), F32), cos, cos, z32], axis=1)
    sin_a = jnp.concatenate([z64, -sin, z16, z32], axis=1)
    sin_b = jnp.concatenate([z64, z16, sin, z32], axis=1)
    return cos_t, sin_a, sin_b


def _pad_heads(w, width):
    K = w.shape[0]
    w = w.reshape(K, MLA_HEADS, width)
    w = jnp.pad(w, ((0, 0), (0, 0), (0, HEAD_PAD - width)))
    return w.reshape(K, MLA_HEADS * HEAD_PAD)


def _block_diag(w):
    a, b = w.shape[1], w.shape[2]
    w = w.reshape(S5_NBLK, S5_BLK_GROUPS, a, b)
    eye = jnp.eye(S5_BLK_GROUPS, dtype=w.dtype)
    out = jnp.einsum('jgab,gh->jgahb', w, eye)
    return out.reshape(S5_NBLK, S5_BLK_GROUPS * a, S5_BLK_GROUPS * b)


def _layer_weights(l, p, disc):
    a_re, a_im, f_re, f_im = [d[l] for d in disc]
    lw = {}
    w_in = p['w_in'][l]
    kpe_cols = jnp.pad(w_in[:, 384:416], ((0, 0), (MLA_NOPE, HEAD_PAD - MLA_NOPE - MLA_ROPE)))
    lw['w1'] = jnp.concatenate([w_in[:, 0:384], kpe_cols, w_in[:, 416:]], axis=1).astype(BF16)
    row = lambda v: v.reshape(1, -1).astype(F32)
    lw['g_mix'] = row(p['g_mix'][l])
    lw['g_q_lat'] = row(p['g_q_lat'][l])
    lw['g_kv_lat'] = row(p['g_kv_lat'][l])
    lw['wuq'] = _pad_heads(p['w_uq'][l], MLA_NOPE + MLA_ROPE).astype(BF16)
    lw['wuk'] = _pad_heads(p['w_uk'][l], MLA_NOPE).astype(BF16)
    lw['wukt'] = p['w_uk'][l].T.astype(BF16)
    lw['wuv'] = p['w_uv'][l].astype(BF16)
    zpad = jnp.zeros((HEAD_PAD - MLA_NOPE - MLA_ROPE,), F32)
    lw['gq'] = row(jnp.concatenate([p['g_qn_nope'][l], p['g_qn_rope'][l], zpad]))
    lw['gk'] = row(jnp.concatenate([p['g_kn_nope'][l], jnp.zeros((HEAD_PAD - MLA_NOPE,), F32)]))
    lw['gkr'] = row(jnp.concatenate([jnp.zeros((MLA_NOPE,), F32), p['g_kn_rope'][l], zpad]))
    br, bi = p['s5_b_re'][l], p['s5_b_im'][l]
    bb_re = f_re[..., None] * br - f_im[..., None] * bi
    bb_im = f_re[..., None] * bi + f_im[..., None] * br
    lw['bb_re'] = _block_diag(jnp.swapaxes(bb_re, 1, 2)).astype(BF16)
    lw['bb_im'] = _block_diag(jnp.swapaxes(bb_im, 1, 2)).astype(BF16)
    lw['cc_re'] = _block_diag(jnp.swapaxes(p['s5_c_re'][l], 1, 2)).astype(BF16)
    lw['cc_im'] = _block_diag(jnp.swapaxes(p['s5_c_im'][l], 1, 2)).astype(BF16)
    lw['a_re'] = a_re.reshape(1, S5_LANES)
    lw['a_im'] = a_im.reshape(1, S5_LANES)
    lw['s5_d'] = row(p['s5_d'][l])
    lw['w_glu'] = p['w_glu'][l].astype(BF16)
    lw['b_glu'] = row(p['b_glu'][l])
    lw['w_out'] = p['w_out'][l].astype(BF16)
    lw['g_ffn'] = row(p['g_ffn'][l])
    wr = jnp.pad(p['w_router'][l], ((0, 0), (0, ROUTER_PAD - N_EXPERTS)))
    lw['wr_hi'] = wr.astype(BF16)
    lw['wr_lo'] = (wr - lw['wr_hi'].astype(F32)).astype(BF16)
    lw['b_router'] = row(jnp.concatenate([p['b_router'][l], jnp.full((ROUTER_PAD - N_EXPERTS,), NEG_BIG, F32)]))
    w_gu = p['w_gu'][l].reshape(N_EXPERTS, D_MODEL, D_FF, 2)
    lw['w_gate'] = w_gu[..., 0].astype(BF16)
    lw['w_up'] = w_gu[..., 1].astype(BF16)
    lw['w_down'] = p['w_down']
    b_gu = p['b_gu'][l].reshape(N_EXPERTS, 1, D_FF, 2)
    lw['b_gate'] = b_gu[..., 0]
    lw['b_up'] = b_gu[..., 1]
    lw['b_down'] = p['b_down'][l].reshape(N_EXPERTS, 1, D_MODEL)
    return lw


def _split_mod(mod):
    return [mod[..., i * D_MODEL:(i + 1) * D_MODEL] for i in range(N_MOD)]


def _layer_pair(xp, xs, mods_p, mods_s, tabs_p, tabs_s, lw, caches, h0, page_table, p, *, B, T, DB, l):
    sh_a, sc_a, gt_a, sh_f, sc_f, gt_f = mods_p
    tm = min(256, T)
    n = B * T
    bm = min(256, max(16, (n * TOP_K) // N_EXPERTS))
    q, k, v, ckv, kpe, u, sq, sk16, sv16, sk, sv = _proj(xp, sc_a, sh_a, tabs_p, lw, B=B, T=T, tm=tm, qk_dtype=BF16)
    o_a = _mla_prompt(q, k, v, B=B, T=T, tq=512 if T % 512 == 0 else min(256, T))
    o_c = _sb_prompt(sq, sk16, sv16, B=B, T=T, tq=min(256, T))
    zeros = jnp.zeros((B, S5_LANES), F32)
    o_b, s_re, s_im = _s5(u.reshape(T * B, S5_WIDTH), zeros, zeros, lw, B=B, T=T, Tc=min(64, T))
    x1, h2, topi, topg = _outproj(xp, o_a, o_b.reshape(T, B * S5_WIDTH), o_c, gt_a, sc_f, sh_f, lw, B=B, T=T, tm=tm)
    slot_tok, dest, block_expert, n_used = _route(topi, bm)
    st_p = (ckv.reshape(B, T, MLA_KV_LORA), kpe.reshape(B, T, MLA_ROPE),
            sk.reshape(B, T, SB_HEADS, SB_HEAD_DIM), sv.reshape(B, T, SB_HEADS, SB_HEAD_DIM),
            s_re.reshape(B, S5_GROUPS, S5_STATE), s_im.reshape(B, S5_GROUPS, S5_STATE))

    ssh_a, ssc_a, sgt_a, ssh_f, ssc_f, sgt_f = mods_s
    cache_kv, cache_kr, cache_sk, cache_sv = caches
    q, k, v, ckv, kpe, u, sq, sk16, sv16, sk, sv = _proj(xs, ssc_a, ssh_a, tabs_s, lw, B=1, T=DB, tm=DB, qk_dtype=F32)
    q8 = jnp.pad(q.reshape(DB, MLA_HEADS, HEAD_PAD), ((0, 0), (0, 8 - MLA_HEADS), (0, 0)))
    k8 = jnp.pad(k.reshape(DB, MLA_HEADS, HEAD_PAD), ((0, 0), (0, 8 - MLA_HEADS), (0, 0)))
    qg = q8[:, :, :MLA_NOPE] * p['g_kn_nope'][l][None, None, :]
    qm = jnp.einsum('bhd,hg->bhgd', qg, jnp.eye(8, MLA_HEADS, dtype=F32)).reshape(DB, 8, MLA_HEADS * MLA_NOPE)
    qpe = q8[:, :, MLA_NOPE:MLA_NOPE + MLA_ROPE].astype(BF16)
    qs = jnp.einsum('bhd,rh,hg->brgd', sq.reshape(DB, SB_HEADS, SB_HEAD_DIM), jnp.eye(8, SB_HEADS, dtype=BF16),
                    jnp.eye(SB_HEADS, dtype=BF16)).reshape(DB, 8, SB_WIDTH)

    xrows = _gather_rows(h2, slot_tok)
    so_c = _sb_decode(page_table, qs, cache_sk, cache_sv, slot_tok, layer=l).reshape(DB, SB_WIDTH)
    ys = _moe(block_expert, n_used, xrows, lw, so_c, bm=bm, layer=l)
    yk = _gather_rows(ys, dest.reshape(n, TOP_K).T.reshape(n * TOP_K)).reshape(TOP_K, n, D_MODEL)
    so_a = _mla_decode(page_table, qm.astype(BF16), qpe, q8.astype(BF16), k8.astype(BF16),
                       ckv.reshape(DB, 1, MLA_KV_LORA), lw, cache_kv, cache_kr, ys,
                       layer=l).reshape(DB, MLA_HEADS * MLA_V)
    xp = _combine(x1, yk, topg, gt_f, so_a, B=B, T=T, tm=512 if T % 512 == 0 else tm)

    so_b, ss_re, ss_im = _s5(u, h0[0], h0[1], lw, B=DB, T=1, Tc=1)
    sx1, sh2, stopi, stopg = _outproj(xs, so_a, so_b, so_c, sgt_a, ssc_f, ssh_f, lw, B=1, T=DB, tm=DB)
    s_tok, s_dest, s_be, s_nu = _route(stopi, 16)
    sys_ = _moe(s_be, s_nu, jnp.take(sh2, s_tok, axis=0), lw, s_nu, bm=16, layer=l)
    syk = jnp.take(sys_, s_dest.reshape(DB, TOP_K).T.reshape(DB * TOP_K), axis=0).reshape(TOP_K, DB, D_MODEL)
    xs = _combine(sx1, syk, stopg, sgt_f, s_nu, B=1, T=DB, tm=DB)
    st_s = (ckv.reshape(DB, 1, MLA_KV_LORA), kpe.reshape(DB, 1, MLA_ROPE),
            sk.reshape(DB, 1, SB_HEADS, SB_HEAD_DIM), sv.reshape(DB, 1, SB_HEADS, SB_HEAD_DIM),
            ss_re.reshape(DB, S5_GROUPS, S5_STATE), ss_im.reshape(DB, S5_GROUPS, S5_STATE))
    return xp, xs, st_p, st_s


def kernel(x_prompt, x_sample, cache_kv_latent, cache_k_rope, cache_sb_k, cache_sb_v, state_s5_re, state_s5_im, page_table, c_prompt, c_sample, w_ada, b_ada, g_mix, w_in, g_q_lat, w_uq, g_qn_nope, g_qn_rope, g_kv_lat, g_kn_rope, w_uk, w_uv, g_kn_nope, s5_lam_re, s5_lam_im, s5_log_step, s5_b_re, s5_b_im, s5_c_re, s5_c_im, s5_d, w_glu, b_glu, w_out, g_ffn, w_router, b_router, w_gu, b_gu, w_down, b_down):
    B, T, _ = x_prompt.shape
    DB, DT, _ = x_sample.shape
    assert DT == 1, "the sample group carries one new token per row"
    L = w_ada.shape[0]
    n_pool = cache_kv_latent.shape[1]
    past_len = page_table.shape[1] * PAGE_SIZE
    p = dict(w_in=w_in, g_mix=g_mix, g_q_lat=g_q_lat, w_uq=w_uq, g_qn_nope=g_qn_nope, g_qn_rope=g_qn_rope,
             g_kv_lat=g_kv_lat, g_kn_rope=g_kn_rope, w_uk=w_uk, w_uv=w_uv, g_kn_nope=g_kn_nope,
             s5_b_re=s5_b_re, s5_b_im=s5_b_im, s5_c_re=s5_c_re, s5_c_im=s5_c_im, s5_d=s5_d, w_glu=w_glu,
             b_glu=b_glu, w_out=w_out, g_ffn=g_ffn, w_router=w_router, b_router=b_router, w_gu=w_gu, b_gu=b_gu,
             w_down=w_down, b_down=b_down)
    mod = _ada(jnp.concatenate([c_prompt, c_sample], axis=0), w_ada, b_ada)
    disc = _s5disc(s5_lam_re, s5_lam_im, s5_log_step)
    tabs_p = _rope_tables(jnp.arange(T, dtype=jnp.int32))
    tabs_s = _rope_tables(jnp.full((DB,), past_len, jnp.int32))
    caches = (cache_kv_latent, jnp.transpose(cache_k_rope, (0, 1, 3, 2)),
              jnp.transpose(cache_sb_k, (0, 1, 3, 4, 2)).reshape(L, n_pool, SB_WIDTH, PAGE_SIZE),
              jnp.transpose(cache_sb_v, (0, 1, 3, 4, 2)).reshape(L, n_pool, SB_WIDTH, PAGE_SIZE))
    xp = x_prompt.reshape(B * T, D_MODEL)
    xs = x_sample.reshape(DB, D_MODEL)
    p_rows, s_rows = [], []
    for l in range(L):
        lw = _layer_weights(l, p, disc)
        mods_p = _split_mod(mod[l, :B].reshape(B, 1, N_MOD * D_MODEL))
        mods_s = _split_mod(mod[l, B:].reshape(1, DB, N_MOD * D_MODEL))
        h0 = (state_s5_re[l].reshape(DB, S5_LANES), state_s5_im[l].reshape(DB, S5_LANES))
        xp, xs, st_p, st_s = _layer_pair(xp, xs, mods_p, mods_s, tabs_p, tabs_s, lw, caches, h0, page_table, p,
                                         B=B, T=T, DB=DB, l=l)
        p_rows.append(st_p)
        s_rows.append(st_s)
    p_out = [jnp.stack(a) for a in zip(*p_rows)]
    s_out = [jnp.stack(a) for a in zip(*s_rows)]
    return (xp.reshape(B, T, D_MODEL), xs.reshape(DB, 1, D_MODEL), *p_out, *s_out)
```
